```python
import math
import jax
import jax.numpy as jnp
from jax import lax
import numpy as np

D_MODEL = 1024
BATCH = 2
SEQ = 16384
DEPTH = 2

CHUNK = 64
Q_BLOCK = 128
SB_HEADS = 4
SB_HEAD_DIM = 64
SB_WIDTH = SB_HEADS * SB_HEAD_DIM
SSM_HEADS = 8
SSM_HEAD_DIM = 64
SSM_WIDTH = SSM_HEADS * SSM_HEAD_DIM
SSM_GROUPS = 2
SSM_STATE = 64
SSM_CONV = 4
SSM_CONV_DIM = SSM_WIDTH + 2 * SSM_GROUPS * SSM_STATE
SC_WIDTH = 256
SC_CONV = 3
MIX_WIDTH = SB_WIDTH + SSM_WIDTH + SC_WIDTH
IN_SEGMENTS = (SB_WIDTH, SB_WIDTH, SB_WIDTH,
               SSM_WIDTH, SSM_CONV_DIM, SSM_HEADS,
               SC_WIDTH, SC_WIDTH, SC_WIDTH)
IN_WIDTH = sum(IN_SEGMENTS)
PEER_HEADS = 4
PEER_KEY_DIM = 256
PEER_NKEYS = 128
PEER_TOPK = 16
PEER_EXPERTS = PEER_NKEYS * PEER_NKEYS
TOKEN_BLOCK = 128
EPS = 1e-6

kernel_name = 'hybrid_stickbreak_ssd_shortconv_peer'


def rmsnorm(x, w):
    x32 = x.astype(jnp.float32)
    y = x32 * lax.rsqrt(jnp.mean(x32 * x32, axis=-1, keepdims=True) + EPS)
    return (y * w.astype(jnp.float32)).astype(x.dtype)


def causal_depthwise_conv(x, w):
    k, c = w.shape
    return lax.conv_general_dilated(
        x, w[:, None, :].astype(x.dtype), window_strides=(1,), padding=[(k - 1, 0)],
        dimension_numbers=('NWC', 'WIO', 'NWC'), feature_group_count=c)


def stick_breaking_attention(q, k, v):
    b, s, h, dh = q.shape
    nblk = s // Q_BLOCK
    scale = dh ** -0.5

    def blocks(a):
        return a.astype(jnp.float32).reshape(b, nblk, Q_BLOCK, h, dh).transpose(1, 0, 3, 2, 4)

    qb, kb, vb = blocks(q), blocks(k), blocks(v)
    qi_np = np.concatenate([np.full(i + 1, i, np.int32) for i in range(nblk)])
    kj_np = np.concatenate([np.arange(i, -1, -1, dtype=np.int32) for i in range(nblk)])
    idx = jnp.arange(Q_BLOCK)
    diag_mask = idx[None, :] < idx[:, None]
    upper = (idx[:, None] > idx[None, :]).astype(jnp.float32)

    def step(carry, ij):
        acc, out_acc, buf = carry
        i, j = ij
        first = j == i
        z = jnp.einsum('bhqd,bhkd->bhqk', qb[i], kb[j]) * scale
        mask = diag_mask | (j < i)
        l = jax.nn.log_sigmoid(-z)
        lm = jnp.where(mask, l, 0.0)
        acc = jnp.where(first, 0.0, acc)
        out_acc = jnp.where(first, 0.0, out_acc)
        rev = jnp.einsum('bhqj,js->bhqs', lm, upper) + acc[..., None]
        w = jnp.where(mask, jnp.exp(l + z + rev), 0.0)
        out_acc = out_acc + jnp.einsum('bhqk,bhkd->bhqd', w, vb[j])
        acc = acc + jnp.sum(lm, axis=-1)
        buf = lax.dynamic_update_index_in_dim(buf, out_acc, i, 0)
        return (acc, out_acc, buf), None

    init = (jnp.zeros((b, h, Q_BLOCK), jnp.float32),
            jnp.zeros((b, h, Q_BLOCK, dh), jnp.float32),
            jnp.zeros((nblk, b, h, Q_BLOCK, dh), jnp.float32))
    (_, _, out), _ = lax.scan(step, init, (jnp.asarray(qi_np), jnp.asarray(kj_np)))
    return out.transpose(1, 0, 3, 2, 4).reshape(b, s, h, dh)


def segsum(x):
    L = x.shape[-1]
    idx = jnp.arange(L)
    xx = jnp.broadcast_to(x[..., :, None], x.shape + (L,))
    xx = jnp.where(idx[:, None] > idx[None, :], xx, 0.0)
    ss = jnp.cumsum(xx, axis=-2)
    return jnp.where(idx[:, None] >= idx[None, :], ss, -jnp.inf)


def ssd_scan(x, dt, A, Bm, Cm):
    b, s, h, p = x.shape
    n = Bm.shape[-1]
    nc = s // CHUNK
    xd = (x * dt[..., None]).reshape(b, nc, CHUNK, h, p)
    a = (dt * A).reshape(b, nc, CHUNK, h).transpose(0, 3, 1, 2)
    Bc = Bm.reshape(b, nc, CHUNK, h, n)
    Cc = Cm.reshape(b, nc, CHUNK, h, n)
    a_cs = jnp.cumsum(a, axis=-1)
    Lmat = jnp.exp(segsum(a))
    cb = jnp.einsum('bclhn,bcshn->bhcls', Cc, Bc)
    y_diag = jnp.einsum('bhcls,bcshp->bclhp', cb * Lmat, xd)
    decay = jnp.exp(a_cs[..., -1:] - a_cs)
    states = jnp.einsum('bclhn,bhcl,bclhp->bchpn', Bc, decay, xd)
    chunk_decay = jnp.exp(a_cs[..., -1])

    def step(carry, inp):
        st, dec = inp
        return carry * dec[..., None, None] + st, carry

    init = jnp.zeros((b, h, p, n), jnp.float32)
    _, prev = lax.scan(step, init, (states.transpose(1, 0, 2, 3, 4), chunk_decay.transpose(2, 0, 1)))
    prev = prev.transpose(1, 0, 2, 3, 4)
    y_off = jnp.einsum('bclhn,bchpn,bhcl->bclhp', Cc, prev, jnp.exp(a_cs))
    return (y_diag + y_off).reshape(b, s, h, p)


def hybrid_mixer(h, w_in, ssm_conv_w, ssm_conv_b, ssm_dt_bias, ssm_A_log, ssm_D, ssm_norm_w, sc_conv_w, w_out):
    b, s, _ = h.shape
    proj = jnp.einsum('bsd,de->bse', h, w_in)
    cuts = []
    acc = 0
    for width in IN_SEGMENTS[:-1]:
        acc += width
        cuts.append(acc)
    q, k, v, z, xbc, dt_raw, sc_b, sc_c, sc_v = jnp.split(proj, cuts, axis=-1)

    y_sb = stick_breaking_attention(q.reshape(b, s, SB_HEADS, SB_HEAD_DIM),
                                    k.reshape(b, s, SB_HEADS, SB_HEAD_DIM),
                                    v.reshape(b, s, SB_HEADS, SB_HEAD_DIM)).reshape(b, s, SB_WIDTH)

    xbc = jax.nn.silu(causal_depthwise_conv(xbc, ssm_conv_w) + ssm_conv_b.astype(xbc.dtype))
    xbc = xbc.astype(jnp.float32)
    xs = xbc[..., :SSM_WIDTH].reshape(b, s, SSM_HEADS, SSM_HEAD_DIM)
    rep = SSM_HEADS // SSM_GROUPS
    Bm = jnp.repeat(xbc[..., SSM_WIDTH:SSM_WIDTH + SSM_GROUPS * SSM_STATE].reshape(b, s, SSM_GROUPS, SSM_STATE), rep, axis=2)
    Cm = jnp.repeat(xbc[..., SSM_WIDTH + SSM_GROUPS * SSM_STATE:].reshape(b, s, SSM_GROUPS, SSM_STATE), rep, axis=2)
    dt = jax.nn.softplus(dt_raw.astype(jnp.float32) + ssm_dt_bias.astype(jnp.float32))
    A = -jnp.exp(ssm_A_log.astype(jnp.float32))
    y_ssm = ssd_scan(xs, dt, A, Bm, Cm) + ssm_D.astype(jnp.float32)[:, None] * xs
    y_ssm = y_ssm.reshape(b, s, SSM_WIDTH) * jax.nn.silu(z.astype(jnp.float32))
    y_ssm = rmsnorm(y_ssm, ssm_norm_w)

    y_sc = sc_b * causal_depthwise_conv(sc_c * sc_v, sc_conv_w)

    y = jnp.concatenate([y_sb.astype(h.dtype), y_ssm.astype(h.dtype), y_sc.astype(h.dtype)], axis=-1)
    return jnp.einsum('bse,ed->bsd', y, w_out)


def peer_ffn(h, w_q, sub_keys, u, v):
    b, s, d = h.shape
    half = PEER_KEY_DIM // 2
    q = jnp.einsum('bsd,de->bse', h, w_q).astype(jnp.float32).reshape(b, s, PEER_HEADS, 2, half)
    scores = jnp.einsum('bshtk,tnk->bshtn', q, sub_keys.astype(jnp.float32))
    s_top, i_top = lax.top_k(scores, PEER_TOPK)
    cand = s_top[..., 0, :, None] + s_top[..., 1, None, :]
    cand_idx = i_top[..., 0, :, None] * PEER_NKEYS + i_top[..., 1, None, :]
    kk = PEER_TOPK * PEER_TOPK
    best, pos = lax.top_k(cand.reshape(b, s, PEER_HEADS, kk), PEER_TOPK)
    expert = jnp.take_along_axis(cand_idx.reshape(b, s, PEER_HEADS, kk), pos, axis=-1)
    gate = jax.nn.softmax(best, axis=-1)

    nblk = s // TOKEN_BLOCK
    hb = h.reshape(b, nblk, TOKEN_BLOCK, d).transpose(1, 0, 2, 3)
    eb = expert.reshape(b, nblk, TOKEN_BLOCK, PEER_HEADS, PEER_TOPK).transpose(1, 0, 2, 3, 4)
    gb = gate.reshape(b, nblk, TOKEN_BLOCK, PEER_HEADS, PEER_TOPK).transpose(1, 0, 2, 3, 4)

    def block(args):
        hx, ex, gx = args
        ue = u[ex]
        act = jax.nn.gelu(jnp.einsum('btd,bthkd->bthk', hx, ue).astype(jnp.float32), approximate=False)
        ve = v[ex]
        return jnp.einsum('bthk,bthkd->btd', (gx * act).astype(ve.dtype), ve)

    out = lax.map(block, (hb, eb, gb))
    return out.transpose(1, 0, 2, 3).reshape(b, s, d).astype(h.dtype)


def setup_inputs(seed: int = 0) -> dict:
    key = jax.random.key(seed)
    ks = jax.random.split(key, 20)
    f32 = jnp.float32

    def nrm(k, shape, scale):
        return jax.random.normal(k, shape, f32) * scale

    dt = jnp.exp(jax.random.uniform(ks[6], (DEPTH, SSM_HEADS), f32, math.log(1e-3), math.log(1e-1)))
    return {
        'x': nrm(ks[0], (BATCH, SEQ, D_MODEL), 1.0),
        'norm1_w': 1.0 + nrm(ks[1], (DEPTH, D_MODEL), 0.05),
        'w_in': nrm(ks[2], (DEPTH, D_MODEL, IN_WIDTH), D_MODEL ** -0.5),
        'ssm_conv_w': nrm(ks[3], (DEPTH, SSM_CONV, SSM_CONV_DIM), SSM_CONV ** -0.5),
        'ssm_conv_b': nrm(ks[4], (DEPTH, SSM_CONV_DIM), 0.02),
        'ssm_dt_bias': dt + jnp.log(-jnp.expm1(-dt)),
        'ssm_A_log': jnp.log(jax.random.uniform(ks[7], (DEPTH, SSM_HEADS), f32, 1.0, 16.0)),
        'ssm_D': 1.0 + nrm(ks[8], (DEPTH, SSM_HEADS), 0.1),
        'ssm_norm_w': 1.0 + nrm(ks[9], (DEPTH, SSM_WIDTH), 0.05),
        'sc_conv_w': nrm(ks[10], (DEPTH, SC_CONV, SC_WIDTH), SC_CONV ** -0.5),
        'w_out': nrm(ks[11], (DEPTH, MIX_WIDTH, D_MODEL), MIX_WIDTH ** -0.5),
        'norm2_w': 1.0 + nrm(ks[12], (DEPTH, D_MODEL), 0.05),
        'peer_w_q': nrm(ks[13], (DEPTH, D_MODEL, PEER_HEADS * PEER_KEY_DIM), D_MODEL ** -0.5),
        'peer_sub_keys': nrm(ks[14], (DEPTH, 2, PEER_NKEYS, PEER_KEY_DIM // 2), (PEER_KEY_DIM // 2) ** -0.5),
        'peer_u': nrm(ks[15], (DEPTH, PEER_EXPERTS, D_MODEL), D_MODEL ** -0.5),
        'peer_v': nrm(ks[16], (DEPTH, PEER_EXPERTS, D_MODEL), 0.5),
        'final_norm_w': 1.0 + nrm(ks[17], (D_MODEL,), 0.05),
    }


def reference(x, norm1_w, w_in, ssm_conv_w, ssm_conv_b, ssm_dt_bias, ssm_A_log, ssm_D, ssm_norm_w,
              sc_conv_w, w_out, norm2_w, peer_w_q, peer_sub_keys, peer_u, peer_v, final_norm_w):
    for i in range(DEPTH):
        h = rmsnorm(x, norm1_w[i])
        x = x + hybrid_mixer(h, w_in[i], ssm_conv_w[i], ssm_conv_b[i], ssm_dt_bias[i], ssm_A_log[i],
                             ssm_D[i], ssm_norm_w[i], sc_conv_w[i], w_out[i])
        h = rmsnorm(x, norm2_w[i])
        x = x + peer_ffn(h, peer_w_q[i], peer_sub_keys[i], peer_u[i], peer_v[i])
    return rmsnorm(x, final_norm_w)
```

```python
import functools
import math

import jax
import jax.numpy as jnp
from jax import lax
from jax.experimental import pallas as pl
from jax.experimental.pallas import tpu as pltpu

F32 = jnp.float32
BF16 = jnp.bfloat16

EPS = 1e-6
D_MODEL = 1024
SB_HEADS = 4
SB_HEAD_DIM = 64
SB_WIDTH = SB_HEADS * SB_HEAD_DIM
SSM_HEADS = 8
SSM_HEAD_DIM = 64
SSM_WIDTH = SSM_HEADS * SSM_HEAD_DIM
SSM_GROUPS = 2
SSM_STATE = 64
SSM_CONV = 4
SSM_CONV_DIM = SSM_WIDTH + 2 * SSM_GROUPS * SSM_STATE
SC_WIDTH = 256
SC_CONV = 3
PEER_HEADS = 4
PEER_KEY_DIM = 256
PEER_NKEYS = 128
PEER_TOPK = 16

LANES = 128
DT_PAD = LANES
REST_WIDTH = SSM_WIDTH + SSM_CONV_DIM + DT_PAD + 3 * SC_WIDTH
QKV_WIDTH = 3 * SB_WIDTH
VMEM_LIMIT = 56 * 1024 * 1024

IN_PROJ_ROWS = 256
SB_BLOCK = 256
MIX_BLOCK = 256
ROUTE_BLOCK = 256
PEER_TOKENS = 512
PEER_EXPERTS_PER_STEP = 1024
SB_SKIP_LOG = -120.0
INV_SQRT2 = 1.0 / math.sqrt(2.0)
NEG_INF = float("-inf")


def _rmsnorm(x, w):
    return x * lax.rsqrt(jnp.mean(x * x, axis=-1, keepdims=True) + EPS) * w


def _softplus(x):
    return jnp.maximum(x, 0.0) + jnp.log(1.0 + jnp.exp(-jnp.abs(x)))


def _silu(x):
    return x * (1.0 / (1.0 + jnp.exp(-x)))


def _params(*semantics):
    return pltpu.CompilerParams(dimension_semantics=semantics, vmem_limit_bytes=VMEM_LIMIT)


def _in_proj_kernel(x_ref, nw_ref, w_ref, rest_ref, qkv_ref):
    h = _rmsnorm(x_ref[...], nw_ref[...]).astype(BF16)
    p = jnp.dot(h, w_ref[...], preferred_element_type=F32)
    rest_ref[...] = p[:, :REST_WIDTH]
    qkv_ref[...] = p[:, REST_WIDTH:].astype(BF16)


def _in_proj(x2, norm_w, w_cat):
    n = x2.shape[0]
    tm = IN_PROJ_ROWS
    width = w_cat.shape[1]
    return pl.pallas_call(
        _in_proj_kernel,
        grid=(n // tm,),
        in_specs=[
            pl.BlockSpec((tm, D_MODEL), lambda i: (i, 0)),
            pl.BlockSpec((1, D_MODEL), lambda i: (0, 0)),
            pl.BlockSpec((D_MODEL, width), lambda i: (0, 0)),
        ],
        out_specs=[
            pl.BlockSpec((tm, REST_WIDTH), lambda i: (i, 0)),
            pl.BlockSpec((tm, QKV_WIDTH), lambda i: (i, 0)),
        ],
        out_shape=[
            jax.ShapeDtypeStruct((n, REST_WIDTH), F32),
            jax.ShapeDtypeStruct((n, QKV_WIDTH), BF16),
        ],
        compiler_params=_params("parallel"),
        name="in_proj",
    )(x2, norm_w, w_cat)


def _sb_attn_kernel(q_ref, k_ref, v_ref, upper_ref, o_ref):
    t = q_ref.shape[0]
    i = pl.program_id(1)
    q = q_ref[...] * (SB_HEAD_DIM ** -0.5)
    upper = upper_ref[...]
    row = lax.broadcasted_iota(jnp.int32, (t, t), 0)
    col = lax.broadcasted_iota(jnp.int32, (t, t), 1)

    def block(j, acc, out, diagonal):
        start = pl.multiple_of(j * t, t)
        kb = k_ref[pl.ds(start, t), :]
        vb = v_ref[pl.ds(start, t), :]
        z = lax.dot_general(q, kb, (((1,), (1,)), ((), ())), preferred_element_type=F32)
        log1m = jnp.minimum(-z, 0.0) - jnp.log(1.0 + jnp.exp(-jnp.abs(z)))
        if diagonal:
            mask = col < row
            log1m_m = jnp.where(mask, log1m, 0.0)
        else:
            log1m_m = log1m
        rev = jnp.dot(log1m_m.astype(BF16), upper, preferred_element_type=F32)
        w = jnp.exp(log1m + z + rev + acc)
        if diagonal:
            w = jnp.where(mask, w, 0.0)
        out = out + jnp.dot(w.astype(BF16), vb, preferred_element_type=F32)
        acc = acc + rev[:, 0:1] + log1m_m[:, 0:1]
        return acc, out

    acc0 = jnp.zeros((t, 1), F32)
    out0 = jnp.zeros((t, q_ref.shape[1]), F32)
    acc0, out0 = block(i, acc0, out0, True)

    def cond(carry):
        j, acc, _ = carry
        return jnp.logical_and(j >= 0, jnp.max(acc) > SB_SKIP_LOG)

    def body(carry):
        j, acc, out = carry
        acc, out = block(j, acc, out, False)
        return j - 1, acc, out

    _, _, out = lax.while_loop(cond, body, (i - 1, acc0, out0))
    o_ref[...] = out


def _sb_attention(q, k, v, upper):
    bh, s, dh = q.shape
    t = SB_BLOCK
    return pl.pallas_call(
        _sb_attn_kernel,
        grid=(bh, s // t),
        in_specs=[
            pl.BlockSpec((None, t, dh), lambda b, i: (b, i, 0)),
            pl.BlockSpec((None, s, dh), lambda b, i: (b, 0, 0)),
            pl.BlockSpec((None, s, dh), lambda b, i: (b, 0, 0)),
            pl.BlockSpec((t, t), lambda b, i: (0, 0)),
        ],
        out_specs=pl.BlockSpec((None, t, dh), lambda b, i: (b, i, 0)),
        out_shape=jax.ShapeDtypeStruct((bh, s, dh), F32),
        compiler_params=_params("parallel", "arbitrary"),
        name="sb_attention",
    )(q, k, v, upper)


def _mixer_kernel(rest_ref, ysb_ref, x_ref, convw_ref, convb_ref, dtb_ref, alog_ref, dexp_ref,
                  normw_ref, scw_ref, wout_ref, tri_ref, o_ref,
                  xbc_ext, sc_ext, state_ref):
    t = rest_ref.shape[0]
    sblk = pl.program_id(1)
    halo = 8

    @pl.when(sblk == 0)
    def _():
        xbc_ext[0:halo, :] = jnp.zeros((halo, SSM_CONV_DIM), F32)
        sc_ext[0:halo, :] = jnp.zeros((halo, SC_WIDTH), F32)
        state_ref[...] = jnp.zeros_like(state_ref)

    o_z = 0
    o_xbc = SSM_WIDTH
    o_dt = o_xbc + SSM_CONV_DIM
    o_sc = o_dt + DT_PAD
    z = rest_ref[:, o_z:o_z + SSM_WIDTH]
    dt_raw = rest_ref[:, o_dt:o_dt + DT_PAD]
    sc_b = rest_ref[:, o_sc:o_sc + SC_WIDTH]
    sc_c = rest_ref[:, o_sc + SC_WIDTH:o_sc + 2 * SC_WIDTH]
    sc_v = rest_ref[:, o_sc + 2 * SC_WIDTH:o_sc + 3 * SC_WIDTH]

    xbc_ext[halo:halo + t, :] = rest_ref[:, o_xbc:o_xbc + SSM_CONV_DIM]
    conv = convb_ref[...]
    for kk in range(SSM_CONV):
        shift = SSM_CONV - 1 - kk
        conv = conv + convw_ref[kk:kk + 1, :] * xbc_ext[halo - shift:halo - shift + t, :]
    tail = xbc_ext[t:t + halo, :]
    xbc_ext[0:halo, :] = tail
    xbc = _silu(conv)
    xs = xbc[:, :SSM_WIDTH]

    sc_ext[halo:halo + t, :] = sc_c * sc_v
    sconv = jnp.zeros((t, SC_WIDTH), F32)
    for kk in range(SC_CONV):
        shift = SC_CONV - 1 - kk
        sconv = sconv + scw_ref[kk:kk + 1, :] * sc_ext[halo - shift:halo - shift + t, :]
    sc_tail = sc_ext[t:t + halo, :]
    sc_ext[0:halo, :] = sc_tail
    y_sc = sc_b * sconv

    dt = _softplus(dt_raw + dtb_ref[...])
    a = dt * (-jnp.exp(alog_ref[...]))
    a_hi = a.astype(BF16)
    a_lo = (a - a_hi.astype(F32)).astype(BF16)
    tri = tri_ref[...]
    a_cs = (jnp.dot(tri, a_hi, preferred_element_type=F32)
            + jnp.dot(tri, a_lo, preferred_element_type=F32))
    a_cs_t = a_cs.T
    a_end = a_cs[t - 1:t, :]
    row = lax.broadcasted_iota(jnp.int32, (t, t), 0)
    col = lax.broadcasted_iota(jnp.int32, (t, t), 1)
    causal = row >= col

    def lanes_per_head(cols, h0, nh):
        return jnp.concatenate(
            [jnp.broadcast_to(cols[:, h:h + 1], (cols.shape[0], SSM_HEAD_DIM)) for h in range(h0, h0 + nh)],
            axis=1)

    hpg = SSM_HEADS // SSM_GROUPS
    gw = hpg * SSM_HEAD_DIM
    lane_head = lax.broadcasted_iota(jnp.int32, (t, gw), 1) // SSM_HEAD_DIM
    y_groups = []
    for g in range(SSM_GROUPS):
        b_g = xbc[:, SSM_WIDTH + g * SSM_STATE:SSM_WIDTH + (g + 1) * SSM_STATE]
        c_g = xbc[:, SSM_WIDTH + SSM_GROUPS * SSM_STATE + g * SSM_STATE:
                  SSM_WIDTH + SSM_GROUPS * SSM_STATE + (g + 1) * SSM_STATE]
        b_bf = b_g.astype(BF16)
        c_bf = c_g.astype(BF16)
        xs_g = xs[:, g * gw:(g + 1) * gw]
        dt_g = lanes_per_head(dt, g * hpg, hpg)
        acs_g = lanes_per_head(a_cs, g * hpg, hpg)
        aend_g = lanes_per_head(a_end, g * hpg, hpg)
        xd_g = xs_g * dt_g
        xd_bf = xd_g.astype(BF16)
        cb = lax.dot_general(c_bf, b_bf, (((1,), (1,)), ((), ())), preferred_element_type=F32)
        y_diag = jnp.zeros((t, gw), F32)
        for hh in range(hpg):
            h = g * hpg + hh
            seg = a_cs[:, h:h + 1] - a_cs_t[h:h + 1, :]
            lmat = jnp.where(causal, jnp.exp(jnp.where(causal, seg, 0.0)), 0.0)
            p_h = jnp.dot((cb * lmat).astype(BF16), xd_bf, preferred_element_type=F32)
            y_diag = jnp.where(lane_head == hh, p_h, y_diag)
        state = state_ref[g]
        y_off = jnp.dot(c_bf, state.astype(BF16), preferred_element_type=F32) * jnp.exp(acs_g)
        xdd = (xd_g * jnp.exp(aend_g - acs_g)).astype(BF16)
        contrib = lax.dot_general(b_bf, xdd, (((0,), (0,)), ((), ())), preferred_element_type=F32)
        state_ref[g] = state * jnp.exp(aend_g) + contrib
        y_groups.append(y_diag + y_off + dexp_ref[:, g * gw:(g + 1) * gw] * xs_g)
    y = jnp.concatenate(y_groups, axis=1) * _silu(z)
    y_ssm = _rmsnorm(y, normw_ref[...])

    o0 = SB_WIDTH
    o1 = o0 + SSM_WIDTH
    out = x_ref[...]
    out = out + jnp.dot(ysb_ref[...].astype(BF16), wout_ref[0:o0, :], preferred_element_type=F32)
    out = out + jnp.dot(y_ssm.astype(BF16), wout_ref[o0:o1, :], preferred_element_type=F32)
    out = out + jnp.dot(y_sc.astype(BF16), wout_ref[o1:o1 + SC_WIDTH, :], preferred_element_type=F32)
    o_ref[...] = out


def _mixer(rest, ysb, x2, batch, conv_w, conv_b, dt_bias, a_log, d_exp, norm_w, sc_w, w_out, tri):
    n = x2.shape[0]
    t = MIX_BLOCK
    spb = n // batch // t
    row_map = lambda b, s: (b * spb + s, 0)
    const = lambda b, s: (0, 0)
    return pl.pallas_call(
        _mixer_kernel,
        grid=(batch, spb),
        in_specs=[
            pl.BlockSpec((t, REST_WIDTH), row_map),
            pl.BlockSpec((t, SB_WIDTH), row_map),
            pl.BlockSpec((t, D_MODEL), row_map),
            pl.BlockSpec((SSM_CONV, SSM_CONV_DIM), const),
            pl.BlockSpec((1, SSM_CONV_DIM), const),
            pl.BlockSpec((1, DT_PAD), const),
            pl.BlockSpec((1, DT_PAD), const),
            pl.BlockSpec((1, SSM_WIDTH), const),
            pl.BlockSpec((1, SSM_WIDTH), const),
            pl.BlockSpec((SC_CONV, SC_WIDTH), const),
            pl.BlockSpec((D_MODEL, D_MODEL), const),
            pl.BlockSpec((t, t), const),
        ],
        out_specs=pl.BlockSpec((t, D_MODEL), row_map),
        out_shape=jax.ShapeDtypeStruct((n, D_MODEL), F32),
        scratch_shapes=[
            pltpu.VMEM((t + 8, SSM_CONV_DIM), F32),
            pltpu.VMEM((t + 8, SC_WIDTH), F32),
            pltpu.VMEM((SSM_GROUPS, SSM_STATE, (SSM_HEADS // SSM_GROUPS) * SSM_HEAD_DIM), F32),
        ],
        compiler_params=_params("arbitrary", "arbitrary"),
        name="mixer",
    )(rest, ysb, x2, conv_w, conv_b, dt_bias, a_log, d_exp, norm_w, sc_w, w_out, tri)


def _top16_rows(x):
    r, t = x.shape
    ridx = lax.broadcasted_iota(jnp.int32, (r, t), 0).astype(F32)
    oidx = lax.broadcasted_iota(jnp.int32, (PEER_TOPK, t), 0)
    out = jnp.zeros((PEER_TOPK, t), F32)
    for k in range(PEER_TOPK):
        m = jnp.max(x, axis=0, keepdims=True)
        first = jnp.min(jnp.where(x == m, ridx, float(r)), axis=0, keepdims=True)
        x = jnp.where(ridx == first, NEG_INF, x)
        out = jnp.where(oidx == k, m, out)
    return out


def _candidate_rows(top0, top1):
    t = top0.shape[1]
    sub = lax.broadcasted_iota(jnp.int32, (8, t), 0)
    blocks = [top0[0:1, :] + top1, top0[1:2, :] + top1[0:8, :]]
    for i in range(2, 8):
        keep = PEER_TOPK // (i + 1)
        blocks.append(jnp.where(sub < keep, top0[i:i + 1, :] + top1[0:8, :], NEG_INF))
    blocks.append(top0[8:16, :] + top1[0:1, :])
    return jnp.concatenate(blocks, axis=0)


def _split_bf16(x):
    hi = x.astype(BF16)
    return hi, (x - hi.astype(F32)).astype(BF16)


def _dot3(a_hi, a_lo, b_hi, b_lo):
    return (jnp.dot(a_hi, b_hi, preferred_element_type=F32)
            + jnp.dot(a_hi, b_lo, preferred_element_type=F32)
            + jnp.dot(a_lo, b_hi, preferred_element_type=F32))


def _route_kernel(x_ref, nw_ref, wqt_hi_ref, wqt_lo_ref, keys_hi_ref, keys_lo_ref,
                  ht_ref, s0_ref, s1_ref, e1_ref, w0_ref, tau_ref):
    h = _rmsnorm(x_ref[...], nw_ref[...])
    ht_hi, ht_lo = _split_bf16(h.T)
    ht_ref[...] = ht_hi
    qt = _dot3(wqt_hi_ref[...], wqt_lo_ref[...], ht_hi, ht_lo)
    half = PEER_KEY_DIM // 2
    for hh in range(PEER_HEADS):
        sc = []
        for side in range(2):
            lo = hh * PEER_KEY_DIM + side * half
            q_hi, q_lo = _split_bf16(qt[lo:lo + half, :])
            sc.append(_dot3(keys_hi_ref[side], keys_lo_ref[side], q_hi, q_lo))
        top0 = _top16_rows(sc[0])
        top1 = _top16_rows(sc[1])
        best = _top16_rows(_candidate_rows(top0, top1))
        z = jnp.sum(jnp.exp(best - best[0:1, :]), axis=0, keepdims=True)
        s0_ref[hh] = sc[0]
        s1_ref[hh] = sc[1]
        e1_ref[hh] = jnp.exp(sc[1] - top1[0:1, :])
        w0_ref[hh] = jnp.exp(sc[0] - top0[0:1, :]) * (0.5 / z)
        tau_ref[hh] = best[PEER_TOPK - 1:PEER_TOPK, :]


def _route(x2, norm_w, wq_t, keys):
    n = x2.shape[0]
    t = ROUTE_BLOCK
    hk = (PEER_HEADS, PEER_NKEYS, n)
    hk_spec = pl.BlockSpec((PEER_HEADS, PEER_NKEYS, t), lambda i: (0, 0, i))
    wq_spec = pl.BlockSpec((PEER_HEADS * PEER_KEY_DIM, D_MODEL), lambda i: (0, 0))
    keys_spec = pl.BlockSpec((2, PEER_NKEYS, PEER_KEY_DIM // 2), lambda i: (0, 0, 0))
    wq_hi, wq_lo = _split_bf16(wq_t)
    keys_hi, keys_lo = _split_bf16(keys)
    return pl.pallas_call(
        _route_kernel,
        grid=(n // t,),
        in_specs=[
            pl.BlockSpec((t, D_MODEL), lambda i: (i, 0)),
            pl.BlockSpec((1, D_MODEL), lambda i: (0, 0)),
            wq_spec, wq_spec, keys_spec, keys_spec,
        ],
        out_specs=[
            pl.BlockSpec((D_MODEL, t), lambda i: (0, i)),
            hk_spec, hk_spec, hk_spec, hk_spec,
            pl.BlockSpec((PEER_HEADS, 1, t), lambda i: (0, 0, i)),
        ],
        out_shape=[
            jax.ShapeDtypeStruct((D_MODEL, n), BF16),
            jax.ShapeDtypeStruct(hk, F32), jax.ShapeDtypeStruct(hk, F32),
            jax.ShapeDtypeStruct(hk, F32), jax.ShapeDtypeStruct(hk, F32),
            jax.ShapeDtypeStruct((PEER_HEADS, 1, n), F32),
        ],
        compiler_params=_params("parallel"),
        name="peer_route",
    )(x2, norm_w, wq_hi, wq_lo, keys_hi, keys_lo)


def _expert_kernel(ht_ref, s0_ref, s1_ref, e1_ref, w0_ref, tau_ref, u_ref, vt_ref, x_ref, fw_ref,
                   o_ref, acc_ref, *, final_norm):
    e = pl.program_id(1)
    eb = u_ref.shape[0]
    nsub = eb // PEER_NKEYS

    @pl.when(e == 0)
    def _():
        acc_ref[...] = jnp.zeros_like(acc_ref)

    st = jnp.dot(u_ref[...], ht_ref[...], preferred_element_type=F32)
    rows = []
    for sub in range(nsub):
        a = e * nsub + sub
        s_sub = st[sub * PEER_NKEYS:(sub + 1) * PEER_NKEYS, :]
        gate = jnp.zeros_like(s_sub)
        for hh in range(PEER_HEADS):
            pair = s0_ref[hh, pl.ds(a, 1), :] + s1_ref[hh]
            picked = jnp.where(pair >= tau_ref[hh], e1_ref[hh], 0.0)
            gate = gate + picked * w0_ref[hh, pl.ds(a, 1), :]
        act = s_sub * (1.0 + lax.erf(s_sub * INV_SQRT2))
        rows.append((gate * act).astype(BF16))
    weighted = jnp.concatenate(rows, axis=0)
    acc_ref[...] += jnp.dot(vt_ref[...], weighted, preferred_element_type=F32)

    @pl.when(e == pl.num_programs(1) - 1)
    def _():
        out = x_ref[...] + acc_ref[...].T
        if final_norm:
            out = _rmsnorm(out, fw_ref[...])
        o_ref[...] = out


def _experts(ht, s0, s1, e1, w0, tau, u_bf, vt_bf, x2, final_w, final_norm):
    n = x2.shape[0]
    t = PEER_TOKENS
    eb = PEER_EXPERTS_PER_STEP
    ne = u_bf.shape[0] // eb
    hk_spec = pl.BlockSpec((PEER_HEADS, PEER_NKEYS, t), lambda i, e: (0, 0, i))
    return pl.pallas_call(
        functools.partial(_expert_kernel, final_norm=final_norm),
        grid=(n // t, ne),
        in_specs=[
            pl.BlockSpec((D_MODEL, t), lambda i, e: (0, i)),
            hk_spec, hk_spec, hk_spec, hk_spec,
            pl.BlockSpec((PEER_HEADS, 1, t), lambda i, e: (0, 0, i)),
            pl.BlockSpec((eb, D_MODEL), lambda i, e: (e, 0)),
            pl.BlockSpec((D_MODEL, eb), lambda i, e: (0, e)),
            pl.BlockSpec((t, D_MODEL), lambda i, e: (i, 0)),
            pl.BlockSpec((1, D_MODEL), lambda i, e: (0, 0)),
        ],
        out_specs=pl.BlockSpec((t, D_MODEL), lambda i, e: (i, 0)),
        out_shape=jax.ShapeDtypeStruct((n, D_MODEL), F32),
        scratch_shapes=[pltpu.VMEM((D_MODEL, t), F32)],
        compiler_params=_params("parallel", "arbitrary"),
        name="peer_experts",
    )(ht, s0, s1, e1, w0, tau, u_bf, vt_bf, x2, final_w)


def _pad_lanes(v, width):
    return jnp.pad(v, ((0, 0), (0, width - v.shape[1])))


def kernel(x, norm1_w, w_in, ssm_conv_w, ssm_conv_b, ssm_dt_bias, ssm_A_log, ssm_D, ssm_norm_w,
           sc_conv_w, w_out, norm2_w, peer_w_q, peer_sub_keys, peer_u, peer_v, final_norm_w):
    batch, seq, d = x.shape
    depth = w_in.shape[0]
    n = batch * seq
    x2 = x.reshape(n, d)

    cuts = [0]
    for width in (SB_WIDTH, SB_WIDTH, SB_WIDTH, SSM_WIDTH, SSM_CONV_DIM, SSM_HEADS,
                  SC_WIDTH, SC_WIDTH, SC_WIDTH):
        cuts.append(cuts[-1] + width)
    ident = jnp.arange(SB_BLOCK)
    upper = (ident[:, None] > ident[None, :]).astype(BF16)
    identm = jnp.arange(MIX_BLOCK)
    tri = (identm[None, :] <= identm[:, None]).astype(BF16)

    for layer in range(depth):
        w = w_in[layer]
        w_cat = jnp.concatenate([
            w[:, cuts[3]:cuts[5]],
            _pad_lanes(w[:, cuts[5]:cuts[6]], DT_PAD),
            w[:, cuts[6]:cuts[9]],
            w[:, cuts[0]:cuts[3]],
        ], axis=1).astype(BF16)
        rest, qkv = _in_proj(x2, norm1_w[layer][None, :], w_cat)

        qkv = qkv.reshape(batch, seq, 3, SB_HEADS, SB_HEAD_DIM).transpose(2, 0, 3, 1, 4)
        qkv = qkv.reshape(3, batch * SB_HEADS, seq, SB_HEAD_DIM)
        ysb = _sb_attention(qkv[0], qkv[1], qkv[2], upper)
        ysb = ysb.reshape(batch, SB_HEADS, seq, SB_HEAD_DIM).transpose(0, 2, 1, 3).reshape(n, SB_WIDTH)

        x2 = _mixer(
            rest, ysb, x2, batch,
            ssm_conv_w[layer], ssm_conv_b[layer][None, :],
            _pad_lanes(ssm_dt_bias[layer][None, :], DT_PAD),
            _pad_lanes(ssm_A_log[layer][None, :], DT_PAD),
            jnp.repeat(ssm_D[layer], SSM_HEAD_DIM)[None, :],
            ssm_norm_w[layer][None, :], sc_conv_w[layer],
            w_out[layer].astype(BF16), tri)

        ht, s0, s1, e1, w0, tau = _route(
            x2, norm2_w[layer][None, :], peer_w_q[layer].T, peer_sub_keys[layer])
        x2 = _experts(ht, s0, s1, e1, w0, tau,
                      peer_u[layer].astype(BF16), peer_v[layer].T.astype(BF16),
                      x2, final_norm_w[None, :], layer == depth - 1)
    return x2.reshape(batch, seq, d)
```

```python
import functools
import math

import jax
import jax.numpy as jnp
from jax import lax
from jax.experimental import pallas as pl
from jax.experimental.pallas import tpu as pltpu

F32 = jnp.float32
BF16 = jnp.bfloat16

EPS = 1e-6
D_MODEL = 1024
SB_HEADS = 4
SB_HEAD_DIM = 64
SB_WIDTH = SB_HEADS * SB_HEAD_DIM
SSM_HEADS = 8
SSM_HEAD_DIM = 64
SSM_WIDTH = SSM_HEADS * SSM_HEAD_DIM
SSM_GROUPS = 2
SSM_STATE = 64
SSM_CONV = 4
SSM_CONV_DIM = SSM_WIDTH + 2 * SSM_GROUPS * SSM_STATE
SC_WIDTH = 256
SC_CONV = 3
PEER_HEADS = 4
PEER_KEY_DIM = 256
PEER_NKEYS = 128
PEER_TOPK = 16

LANES = 128
DT_PAD = LANES
REST_WIDTH = SSM_WIDTH + SSM_CONV_DIM + DT_PAD + 3 * SC_WIDTH
QKV_WIDTH = 3 * SB_WIDTH
VMEM_LIMIT = 56 * 1024 * 1024

IN_PROJ_ROWS = 256
SB_BLOCK = 256
MIX_BLOCK = 256
ROUTE_BLOCK = 256
PEER_TOKENS = 512
PEER_EXPERTS_PER_STEP = 2048
SB_SKIP_LOG = -120.0
INV_SQRT2 = 1.0 / math.sqrt(2.0)
NEG_INF = float("-inf")


def _rmsnorm(x, w):
    return x * lax.rsqrt(jnp.mean(x * x, axis=-1, keepdims=True) + EPS) * w


def _softplus(x):
    return jnp.maximum(x, 0.0) + jnp.log(1.0 + jnp.exp(-jnp.abs(x)))


def _silu(x):
    return x * (1.0 / (1.0 + jnp.exp(-x)))


def _params(*semantics):
    return pltpu.CompilerParams(dimension_semantics=semantics, vmem_limit_bytes=VMEM_LIMIT)


def _in_proj_kernel(x_ref, nw_ref, w_ref, rest_ref, qkv_ref):
    h = _rmsnorm(x_ref[...], nw_ref[...]).astype(BF16)
    p = jnp.dot(h, w_ref[...], preferred_element_type=F32)
    rest_ref[...] = p[:, :REST_WIDTH]
    qkv_ref[...] = p[:, REST_WIDTH:].astype(BF16)


def _in_proj(x2, norm_w, w_cat):
    n = x2.shape[0]
    tm = IN_PROJ_ROWS
    width = w_cat.shape[1]
    return pl.pallas_call(
        _in_proj_kernel,
        grid=(n // tm,),
        in_specs=[
            pl.BlockSpec((tm, D_MODEL), lambda i: (i, 0)),
            pl.BlockSpec((1, D_MODEL), lambda i: (0, 0)),
            pl.BlockSpec((D_MODEL, width), lambda i: (0, 0)),
        ],
        out_specs=[
            pl.BlockSpec((tm, REST_WIDTH), lambda i: (i, 0)),
            pl.BlockSpec((tm, QKV_WIDTH), lambda i: (i, 0)),
        ],
        out_shape=[
            jax.ShapeDtypeStruct((n, REST_WIDTH), F32),
            jax.ShapeDtypeStruct((n, QKV_WIDTH), BF16),
        ],
        compiler_params=_params("parallel"),
        name="in_proj",
    )(x2, norm_w, w_cat)


def _sb_attn_kernel(q_ref, k_ref, v_ref, upper_ref, o_ref):
    t = q_ref.shape[0]
    i = pl.program_id(1)
    q = q_ref[...] * (SB_HEAD_DIM ** -0.5)
    upper = upper_ref[...]
    row = lax.broadcasted_iota(jnp.int32, (t, t), 0)
    col = lax.broadcasted_iota(jnp.int32, (t, t), 1)

    def block(j, acc, out, diagonal):
        start = pl.multiple_of(j * t, t)
        kb = k_ref[pl.ds(start, t), :]
        vb = v_ref[pl.ds(start, t), :]
        z = lax.dot_general(q, kb, (((1,), (1,)), ((), ())), preferred_element_type=F32)
        log1m = jnp.minimum(-z, 0.0) - jnp.log(1.0 + jnp.exp(-jnp.abs(z)))
        if diagonal:
            mask = col < row
            log1m_m = jnp.where(mask, log1m, 0.0)
        else:
            log1m_m = log1m
        rev = jnp.dot(log1m_m.astype(BF16), upper, preferred_element_type=F32)
        w = jnp.exp(log1m + z + rev + acc)
        if diagonal:
            w = jnp.where(mask, w, 0.0)
        out = out + jnp.dot(w.astype(BF16), vb, preferred_element_type=F32)
        acc = acc + rev[:, 0:1] + log1m_m[:, 0:1]
        return acc, out

    acc0 = jnp.zeros((t, 1), F32)
    out0 = jnp.zeros((t, q_ref.shape[1]), F32)
    acc0, out0 = block(i, acc0, out0, True)

    def cond(carry):
        j, acc, _ = carry
        return jnp.logical_and(j >= 0, jnp.max(acc) > SB_SKIP_LOG)

    def body(carry):
        j, acc, out = carry
        acc, out = block(j, acc, out, False)
        return j - 1, acc, out

    _, _, out = lax.while_loop(cond, body, (i - 1, acc0, out0))
    o_ref[...] = out


def _sb_attention(q, k, v, upper):
    bh, s, dh = q.shape
    t = SB_BLOCK
    return pl.pallas_call(
        _sb_attn_kernel,
        grid=(bh, s // t),
        in_specs=[
            pl.BlockSpec((None, t, dh), lambda b, i: (b, i, 0)),
            pl.BlockSpec((None, s, dh), lambda b, i: (b, 0, 0)),
            pl.BlockSpec((None, s, dh), lambda b, i: (b, 0, 0)),
            pl.BlockSpec((t, t), lambda b, i: (0, 0)),
        ],
        out_specs=pl.BlockSpec((None, t, dh), lambda b, i: (b, i, 0)),
        out_shape=jax.ShapeDtypeStruct((bh, s, dh), F32),
        compiler_params=_params("parallel", "arbitrary"),
        name="sb_attention",
    )(q, k, v, upper)


def _mixer_kernel(rest_ref, ysb_ref, x_ref, convw_ref, convb_ref, dtb_ref, alog_ref, dexp_ref,
                  normw_ref, scw_ref, wout_ref, tri_ref, o_ref,
                  xbc_ext, sc_ext, state_ref):
    t = rest_ref.shape[0]
    sblk = pl.program_id(1)
    halo = 8

    @pl.when(sblk == 0)
    def _():
        xbc_ext[0:halo, :] = jnp.zeros((halo, SSM_CONV_DIM), F32)
        sc_ext[0:halo, :] = jnp.zeros((halo, SC_WIDTH), F32)
        state_ref[...] = jnp.zeros_like(state_ref)

    o_z = 0
    o_xbc = SSM_WIDTH
    o_dt = o_xbc + SSM_CONV_DIM
    o_sc = o_dt + DT_PAD
    z = rest_ref[:, o_z:o_z + SSM_WIDTH]
    dt_raw = rest_ref[:, o_dt:o_dt + DT_PAD]
    sc_b = rest_ref[:, o_sc:o_sc + SC_WIDTH]
    sc_c = rest_ref[:, o_sc + SC_WIDTH:o_sc + 2 * SC_WIDTH]
    sc_v = rest_ref[:, o_sc + 2 * SC_WIDTH:o_sc + 3 * SC_WIDTH]

    xbc_ext[halo:halo + t, :] = rest_ref[:, o_xbc:o_xbc + SSM_CONV_DIM]
    conv = convb_ref[...]
    for kk in range(SSM_CONV):
        shift = SSM_CONV - 1 - kk
        conv = conv + convw_ref[kk:kk + 1, :] * xbc_ext[halo - shift:halo - shift + t, :]
    tail = xbc_ext[t:t + halo, :]
    xbc_ext[0:halo, :] = tail
    xbc = _silu(conv)
    xs = xbc[:, :SSM_WIDTH]

    sc_ext[halo:halo + t, :] = sc_c * sc_v
    sconv = jnp.zeros((t, SC_WIDTH), F32)
    for kk in range(SC_CONV):
        shift = SC_CONV - 1 - kk
        sconv = sconv + scw_ref[kk:kk + 1, :] * sc_ext[halo - shift:halo - shift + t, :]
    sc_tail = sc_ext[t:t + halo, :]
    sc_ext[0:halo, :] = sc_tail
    y_sc = sc_b * sconv

    dt = _softplus(dt_raw + dtb_ref[...])
    a = dt * (-jnp.exp(alog_ref[...]))
    a_hi = a.astype(BF16)
    a_lo = (a - a_hi.astype(F32)).astype(BF16)
    tri = tri_ref[...]
    a_cs = (jnp.dot(tri, a_hi, preferred_element_type=F32)
            + jnp.dot(tri, a_lo, preferred_element_type=F32))
    a_cs_t = a_cs.T
    a_end = a_cs[t - 1:t, :]
    row = lax.broadcasted_iota(jnp.int32, (t, t), 0)
    col = lax.broadcasted_iota(jnp.int32, (t, t), 1)
    causal = row >= col

    def lanes_per_head(cols, h0, nh):
        return jnp.concatenate(
            [jnp.broadcast_to(cols[:, h:h + 1], (cols.shape[0], SSM_HEAD_DIM)) for h in range(h0, h0 + nh)],
            axis=1)

    hpg = SSM_HEADS // SSM_GROUPS
    gw = hpg * SSM_HEAD_DIM
    lane_head = lax.broadcasted_iota(jnp.int32, (t, gw), 1) // SSM_HEAD_DIM
    y_groups = []
    for g in range(SSM_GROUPS):
        b_g = xbc[:, SSM_WIDTH + g * SSM_STATE:SSM_WIDTH + (g + 1) * SSM_STATE]
        c_g = xbc[:, SSM_WIDTH + SSM_GROUPS * SSM_STATE + g * SSM_STATE:
                  SSM_WIDTH + SSM_GROUPS * SSM_STATE + (g + 1) * SSM_STATE]
        b_bf = b_g.astype(BF16)
        c_bf = c_g.astype(BF16)
        xs_g = xs[:, g * gw:(g + 1) * gw]
        dt_g = lanes_per_head(dt, g * hpg, hpg)
        acs_g = lanes_per_head(a_cs, g * hpg, hpg)
        aend_g = lanes_per_head(a_end, g * hpg, hpg)
        xd_g = xs_g * dt_g
        xd_bf = xd_g.astype(BF16)
        cb = lax.dot_general(c_bf, b_bf, (((1,), (1,)), ((), ())), preferred_element_type=F32)
        y_diag = jnp.zeros((t, gw), F32)
        for hh in range(hpg):
            h = g * hpg + hh
            seg = a_cs[:, h:h + 1] - a_cs_t[h:h + 1, :]
            lmat = jnp.where(causal, jnp.exp(jnp.where(causal, seg, 0.0)), 0.0)
            p_h = jnp.dot((cb * lmat).astype(BF16), xd_bf, preferred_element_type=F32)
            y_diag = jnp.where(lane_head == hh, p_h, y_diag)
        state = state_ref[g]
        y_off = jnp.dot(c_bf, state.astype(BF16), preferred_element_type=F32) * jnp.exp(acs_g)
        xdd = (xd_g * jnp.exp(aend_g - acs_g)).astype(BF16)
        contrib = lax.dot_general(b_bf, xdd, (((0,), (0,)), ((), ())), preferred_element_type=F32)
        state_ref[g] = state * jnp.exp(aend_g) + contrib
        y_groups.append(y_diag + y_off + dexp_ref[:, g * gw:(g + 1) * gw] * xs_g)
    y = jnp.concatenate(y_groups, axis=1) * _silu(z)
    y_ssm = _rmsnorm(y, normw_ref[...])

    o0 = SB_WIDTH
    o1 = o0 + SSM_WIDTH
    out = x_ref[...]
    out = out + jnp.dot(ysb_ref[...].astype(BF16), wout_ref[0:o0, :], preferred_element_type=F32)
    out = out + jnp.dot(y_ssm.astype(BF16), wout_ref[o0:o1, :], preferred_element_type=F32)
    out = out + jnp.dot(y_sc.astype(BF16), wout_ref[o1:o1 + SC_WIDTH, :], preferred_element_type=F32)
    o_ref[...] = out


def _mixer(rest, ysb, x2, batch, conv_w, conv_b, dt_bias, a_log, d_exp, norm_w, sc_w, w_out, tri):
    n = x2.shape[0]
    t = MIX_BLOCK
    spb = n // batch // t
    row_map = lambda b, s: (b * spb + s, 0)
    const = lambda b, s: (0, 0)
    return pl.pallas_call(
        _mixer_kernel,
        grid=(batch, spb),
        in_specs=[
            pl.BlockSpec((t, REST_WIDTH), row_map),
            pl.BlockSpec((t, SB_WIDTH), row_map),
            pl.BlockSpec((t, D_MODEL), row_map),
            pl.BlockSpec((SSM_CONV, SSM_CONV_DIM), const),
            pl.BlockSpec((1, SSM_CONV_DIM), const),
            pl.BlockSpec((1, DT_PAD), const),
            pl.BlockSpec((1, DT_PAD), const),
            pl.BlockSpec((1, SSM_WIDTH), const),
            pl.BlockSpec((1, SSM_WIDTH), const),
            pl.BlockSpec((SC_CONV, SC_WIDTH), const),
            pl.BlockSpec((D_MODEL, D_MODEL), const),
            pl.BlockSpec((t, t), const),
        ],
        out_specs=pl.BlockSpec((t, D_MODEL), row_map),
        out_shape=jax.ShapeDtypeStruct((n, D_MODEL), F32),
        scratch_shapes=[
            pltpu.VMEM((t + 8, SSM_CONV_DIM), F32),
            pltpu.VMEM((t + 8, SC_WIDTH), F32),
            pltpu.VMEM((SSM_GROUPS, SSM_STATE, (SSM_HEADS // SSM_GROUPS) * SSM_HEAD_DIM), F32),
        ],
        compiler_params=_params("arbitrary", "arbitrary"),
        name="mixer",
    )(rest, ysb, x2, conv_w, conv_b, dt_bias, a_log, d_exp, norm_w, sc_w, w_out, tri)


SUBLANES = 8


def _sort16_pairs():
    def merge(lo, hi, r):
        step = r * 2
        if step < hi - lo:
            yield from merge(lo, hi, step)
            yield from merge(lo + r, hi, step)
            for i in range(lo + r, hi - r, step):
                yield (i, i + r)
        else:
            yield (lo, lo + r)

    def sort(lo, hi):
        if hi - lo >= 1:
            mid = lo + (hi - lo) // 2
            yield from sort(lo, mid)
            yield from sort(mid + 1, hi)
            yield from merge(lo, hi, 1)

    return tuple(sort(0, PEER_TOPK - 1))


_SORT16 = _sort16_pairs()


def _exchange(v, i, j):
    v[i], v[j] = jnp.maximum(v[i], v[j]), jnp.minimum(v[i], v[j])


def _merge_sublane_lists(v):
    for shift in (4, 2, 1):
        r = [pltpu.roll(x, shift, axis=0) for x in v]
        v = [jnp.maximum(v[i], r[PEER_TOPK - 1 - i]) for i in range(PEER_TOPK)]
        d = PEER_TOPK // 2
        while d >= 1:
            for i in range(PEER_TOPK):
                if not i & d:
                    _exchange(v, i, i + d)
            d //= 2
    return v


def _top16_keys(scores):
    v = [scores[k * SUBLANES:(k + 1) * SUBLANES, :] for k in range(PEER_NKEYS // SUBLANES)]
    for i, j in _SORT16:
        _exchange(v, i, j)
    return _merge_sublane_lists(v)


def _top16_pairs(top0, top1):
    t = top0[0].shape[1]
    sub = lax.broadcasted_iota(jnp.int32, (SUBLANES, t), 0)
    lengths = (16, 8, 5, 4, 12, 4, 1, 0)
    length = jnp.zeros((SUBLANES, t), jnp.int32)
    for s_, n_ in enumerate(lengths):
        length = jnp.where(sub == s_, n_, length)
    row_part = jnp.where(sub == 0, top0[0], jnp.where(sub == 1, top0[1],
                                                     jnp.where(sub == 2, top0[2], top0[3])))
    col_part = jnp.where(sub == 4, top1[0], jnp.where(sub == 5, top1[1], top1[2]))
    lists = []
    for r in range(PEER_TOPK):
        a = jnp.where(sub < 4, row_part, top0[min(4 + r, PEER_TOPK - 1)])
        b = jnp.where(sub < 4, top1[r], col_part)
        lists.append(jnp.where(r < length, a + b, NEG_INF))
    return _merge_sublane_lists(lists)


def _split_bf16(x):
    hi = x.astype(BF16)
    return hi, (x - hi.astype(F32)).astype(BF16)


def _dot3(a_hi, a_lo, b_hi, b_lo):
    return (jnp.dot(a_hi, b_hi, preferred_element_type=F32)
            + jnp.dot(a_hi, b_lo, preferred_element_type=F32)
            + jnp.dot(a_lo, b_hi, preferred_element_type=F32))


def _route_kernel(x_ref, nw_ref, wqt_hi_ref, wqt_lo_ref, keys_hi_ref, keys_lo_ref,
                  ht_ref, n0_ref, w0_ref, r1_ref, e1_ref):
    h = _rmsnorm(x_ref[...], nw_ref[...])
    ht_hi, ht_lo = _split_bf16(h.T)
    ht_ref[...] = ht_hi
    qt = _dot3(wqt_hi_ref[...], wqt_lo_ref[...], ht_hi, ht_lo)
    half = PEER_KEY_DIM // 2
    for hh in range(PEER_HEADS):
        sc = []
        for side in range(2):
            lo = hh * PEER_KEY_DIM + side * half
            q_hi, q_lo = _split_bf16(qt[lo:lo + half, :])
            sc.append(_dot3(keys_hi_ref[side], keys_lo_ref[side], q_hi, q_lo))
        for l0 in range(0, sc[0].shape[1], LANES):
            ln = slice(l0, l0 + LANES)
            _route_lane_tile(sc[0][:, ln], sc[1][:, ln], hh, ln, n0_ref, w0_ref, r1_ref, e1_ref)


def _route_lane_tile(s0, s1, hh, ln, n0_ref, w0_ref, r1_ref, e1_ref):
    top0 = _top16_keys(s0)
    top1 = _top16_keys(s1)
    best = _top16_pairs(top0, top1)
    z = jnp.exp(best[0] - best[0])
    for r in range(1, PEER_TOPK):
        z = z + jnp.exp(best[r] - best[0])
    tau = best[PEER_TOPK - 1]
    scale0 = INV_SQRT2 / z
    r1_rows, e1_rows = [], []
    for k in range(PEER_NKEYS // SUBLANES):
        rows = slice(k * SUBLANES, (k + 1) * SUBLANES)
        s0k = s0[rows, :]
        s1k = s1[rows, :]
        n0 = jnp.zeros_like(s0k)
        r1 = jnp.zeros_like(s1k)
        for j in range(PEER_TOPK):
            n0 = n0 + jnp.where(s0k + top1[j] >= tau, 1.0, 0.0)
            r1 = r1 + jnp.where(top1[j] > s1k, 1.0, 0.0)
        n0_ref[hh, rows, ln] = n0
        w0_ref[hh, rows, ln] = jnp.exp(s0k - top0[0]) * scale0
        r1_rows.append(r1)
        e1_rows.append(jnp.exp(s1k - top1[0]))
    r1_ref[hh, :, ln] = jnp.concatenate(r1_rows, axis=0).astype(BF16)
    e1_ref[hh, :, ln] = jnp.concatenate(e1_rows, axis=0).astype(BF16)


def _route(x2, norm_w, wq_t, keys):
    n = x2.shape[0]
    t = ROUTE_BLOCK
    hk = (PEER_HEADS, PEER_NKEYS, n)
    hk_spec = pl.BlockSpec((PEER_HEADS, PEER_NKEYS, t), lambda i: (0, 0, i))
    wq_spec = pl.BlockSpec((PEER_HEADS * PEER_KEY_DIM, D_MODEL), lambda i: (0, 0))
    keys_spec = pl.BlockSpec((2, PEER_NKEYS, PEER_KEY_DIM // 2), lambda i: (0, 0, 0))
    wq_hi, wq_lo = _split_bf16(wq_t)
    keys_hi, keys_lo = _split_bf16(keys)
    return pl.pallas_call(
        _route_kernel,
        grid=(n // t,),
        in_specs=[
            pl.BlockSpec((t, D_MODEL), lambda i: (i, 0)),
            pl.BlockSpec((1, D_MODEL), lambda i: (0, 0)),
            wq_spec, wq_spec, keys_spec, keys_spec,
        ],
        out_specs=[
            pl.BlockSpec((D_MODEL, t), lambda i: (0, i)),
            hk_spec, hk_spec, hk_spec, hk_spec,
        ],
        out_shape=[
            jax.ShapeDtypeStruct((D_MODEL, n), BF16),
            jax.ShapeDtypeStruct(hk, F32), jax.ShapeDtypeStruct(hk, F32),
            jax.ShapeDtypeStruct(hk, BF16), jax.ShapeDtypeStruct(hk, BF16),
        ],
        compiler_params=_params("parallel"),
        name="peer_route",
    )(x2, norm_w, wq_hi, wq_lo, keys_hi, keys_lo)


def _expert_kernel(ht_ref, n0_ref, w0_ref, r1_ref, e1_ref, u_ref, vt_ref, x_ref, fw_ref,
                   o_ref, acc_ref, *, final_norm):
    e = pl.program_id(1)
    eb = u_ref.shape[0]
    t = ht_ref.shape[1]

    @pl.when(e == 0)
    def _():
        acc_ref[...] = jnp.zeros_like(acc_ref)

    y_all = jnp.dot(u_ref[...], ht_ref[...], preferred_element_type=F32)
    rows = []
    for sub in range(eb // PEER_NKEYS):
        y = y_all[sub * PEER_NKEYS:(sub + 1) * PEER_NKEYS, :]
        gate = jnp.zeros((PEER_NKEYS, t), BF16)
        for hh in range(PEER_HEADS):
            n0 = jnp.broadcast_to(n0_ref[hh, sub:sub + 1, :].astype(BF16), (PEER_NKEYS, t))
            w0 = jnp.broadcast_to(w0_ref[hh, sub:sub + 1, :].astype(BF16), (PEER_NKEYS, t))
            gate = gate + jnp.where(r1_ref[hh] < n0, e1_ref[hh], 0.0) * w0
        rows.append(gate * (y * (1.0 + lax.erf(y))).astype(BF16))
    weighted = jnp.concatenate(rows, axis=0)
    acc_ref[...] += jnp.dot(vt_ref[...], weighted, preferred_element_type=F32)

    @pl.when(e == pl.num_programs(1) - 1)
    def _():
        out = x_ref[...] + acc_ref[...].T
        if final_norm:
            out = _rmsnorm(out, fw_ref[...])
        o_ref[...] = out


def _experts(ht, n0, w0, r1, e1, u_bf, vt_bf, x2, final_w, final_norm):
    n = x2.shape[0]
    t = PEER_TOKENS
    eb = PEER_EXPERTS_PER_STEP
    ne = u_bf.shape[0] // eb
    hk_spec = pl.BlockSpec((PEER_HEADS, PEER_NKEYS, t), lambda i, e: (0, 0, i))
    ha_spec = pl.BlockSpec((PEER_HEADS, eb // PEER_NKEYS, t), lambda i, e: (0, e, i))
    return pl.pallas_call(
        functools.partial(_expert_kernel, final_norm=final_norm),
        grid=(n // t, ne),
        in_specs=[
            pl.BlockSpec((D_MODEL, t), lambda i, e: (0, i)),
            ha_spec, ha_spec, hk_spec, hk_spec,
            pl.BlockSpec((eb, D_MODEL), lambda i, e: (e, 0)),
            pl.BlockSpec((D_MODEL, eb), lambda i, e: (0, e)),
            pl.BlockSpec((t, D_MODEL), lambda i, e: (i, 0)),
            pl.BlockSpec((1, D_MODEL), lambda i, e: (0, 0)),
        ],
        out_specs=pl.BlockSpec((t, D_MODEL), lambda i, e: (i, 0)),
        out_shape=jax.ShapeDtypeStruct((n, D_MODEL), F32),
        scratch_shapes=[pltpu.VMEM((D_MODEL, t), F32)],
        compiler_params=_params("parallel", "arbitrary"),
        name="peer_experts",
    )(ht, n0, w0, r1, e1, u_bf, vt_bf, x2, final_w)


def _pad_lanes(v, width):
    return jnp.pad(v, ((0, 0), (0, width - v.shape[1])))


def kernel(x, norm1_w, w_in, ssm_conv_w, ssm_conv_b, ssm_dt_bias, ssm_A_log, ssm_D, ssm_norm_w,
           sc_conv_w, w_out, norm2_w, peer_w_q, peer_sub_keys, peer_u, peer_v, final_norm_w):
    batch, seq, d = x.shape
    depth = w_in.shape[0]
    n = batch * seq
    x2 = x.reshape(n, d)

    cuts = [0]
    for width in (SB_WIDTH, SB_WIDTH, SB_WIDTH, SSM_WIDTH, SSM_CONV_DIM, SSM_HEADS,
                  SC_WIDTH, SC_WIDTH, SC_WIDTH):
        cuts.append(cuts[-1] + width)
    ident = jnp.arange(SB_BLOCK)
    upper = (ident[:, None] > ident[None, :]).astype(BF16)
    identm = jnp.arange(MIX_BLOCK)
    tri = (identm[None, :] <= identm[:, None]).astype(BF16)

    for layer in range(depth):
        w = w_in[layer]
        w_cat = jnp.concatenate([
            w[:, cuts[3]:cuts[5]],
            _pad_lanes(w[:, cuts[5]:cuts[6]], DT_PAD),
            w[:, cuts[6]:cuts[9]],
            w[:, cuts[0]:cuts[3]],
        ], axis=1).astype(BF16)
        rest, qkv = _in_proj(x2, norm1_w[layer][None, :], w_cat)

        qkv = qkv.reshape(batch, seq, 3, SB_HEADS, SB_HEAD_DIM).transpose(2, 0, 3, 1, 4)
        qkv = qkv.reshape(3, batch * SB_HEADS, seq, SB_HEAD_DIM)
        ysb = _sb_attention(qkv[0], qkv[1], qkv[2], upper)
        ysb = ysb.reshape(batch, SB_HEADS, seq, SB_HEAD_DIM).transpose(0, 2, 1, 3).reshape(n, SB_WIDTH)

        x2 = _mixer(
            rest, ysb, x2, batch,
            ssm_conv_w[layer], ssm_conv_b[layer][None, :],
            _pad_lanes(ssm_dt_bias[layer][None, :], DT_PAD),
            _pad_lanes(ssm_A_log[layer][None, :], DT_PAD),
            jnp.repeat(ssm_D[layer], SSM_HEAD_DIM)[None, :],
            ssm_norm_w[layer][None, :], sc_conv_w[layer],
            w_out[layer].astype(BF16), tri)

        ht, n0, w0, r1, e1 = _route(
            x2, norm2_w[layer][None, :], peer_w_q[layer].T, peer_sub_keys[layer])
        x2 = _experts(ht, n0, w0, r1, e1,
                      (peer_u[layer] * INV_SQRT2).astype(BF16), peer_v[layer].T.astype(BF16),
                      x2, final_norm_w[None, :], layer == depth - 1)
    return x2.reshape(batch, seq, d)
```

```python
import functools
import math

import jax
import jax.numpy as jnp
from jax import lax
from jax.experimental import pallas as pl
from jax.experimental.pallas import tpu as pltpu

F32 = jnp.float32
BF16 = jnp.bfloat16

EPS = 1e-6
D_MODEL = 1024
SB_HEADS = 4
SB_HEAD_DIM = 64
SB_WIDTH = SB_HEADS * SB_HEAD_DIM
SSM_HEADS = 8
SSM_HEAD_DIM = 64
SSM_WIDTH = SSM_HEADS * SSM_HEAD_DIM
SSM_GROUPS = 2
SSM_STATE = 64
SSM_CONV = 4
SSM_CONV_DIM = SSM_WIDTH + 2 * SSM_GROUPS * SSM_STATE
SC_WIDTH = 256
SC_CONV = 3
PEER_HEADS = 4
PEER_KEY_DIM = 256
PEER_NKEYS = 128
PEER_TOPK = 16

LANES = 128
DT_PAD = LANES
REST_WIDTH = SSM_WIDTH + SSM_CONV_DIM + DT_PAD + 3 * SC_WIDTH
QKV_WIDTH = 3 * SB_WIDTH
VMEM_LIMIT = 56 * 1024 * 1024

IN_PROJ_ROWS = 256
SB_BLOCK = 256
MIX_BLOCK = 256
ROUTE_BLOCK = 256
PEER_TOKENS = 512
PEER_EXPERTS_PER_STEP = 2048
SB_SKIP_LOG = -120.0
INV_SQRT2 = 1.0 / math.sqrt(2.0)
NEG_INF = float("-inf")


def _rmsnorm(x, w):
    return x * lax.rsqrt(jnp.mean(x * x, axis=-1, keepdims=True) + EPS) * w


def _softplus(x):
    return jnp.maximum(x, 0.0) + jnp.log(1.0 + jnp.exp(-jnp.abs(x)))


def _silu(x):
    return x * (1.0 / (1.0 + jnp.exp(-x)))


def _params(*semantics):
    return pltpu.CompilerParams(dimension_semantics=semantics, vmem_limit_bytes=VMEM_LIMIT)


def _in_proj_kernel(x_ref, nw_ref, w_ref, rest_ref, qkv_ref):
    h = _rmsnorm(x_ref[...], nw_ref[...]).astype(BF16)
    p = jnp.dot(h, w_ref[...], preferred_element_type=F32)
    rest_ref[...] = p[:, :REST_WIDTH]
    qkv_ref[...] = p[:, REST_WIDTH:].astype(BF16)


def _in_proj(x2, norm_w, w_cat):
    n = x2.shape[0]
    tm = IN_PROJ_ROWS
    width = w_cat.shape[1]
    return pl.pallas_call(
        _in_proj_kernel,
        grid=(n // tm,),
        in_specs=[
            pl.BlockSpec((tm, D_MODEL), lambda i: (i, 0)),
            pl.BlockSpec((1, D_MODEL), lambda i: (0, 0)),
            pl.BlockSpec((D_MODEL, width), lambda i: (0, 0)),
        ],
        out_specs=[
            pl.BlockSpec((tm, REST_WIDTH), lambda i: (i, 0)),
            pl.BlockSpec((tm, QKV_WIDTH), lambda i: (i, 0)),
        ],
        out_shape=[
            jax.ShapeDtypeStruct((n, REST_WIDTH), F32),
            jax.ShapeDtypeStruct((n, QKV_WIDTH), BF16),
        ],
        compiler_params=_params("parallel"),
        name="in_proj",
    )(x2, norm_w, w_cat)


def _sb_attn_kernel(q_ref, k_ref, v_ref, upper_ref, o_ref):
    t = q_ref.shape[0]
    i = pl.program_id(1)
    upper = upper_ref[...]
    row = lax.broadcasted_iota(jnp.int32, (t, t), 0)
    col = lax.broadcasted_iota(jnp.int32, (t, t), 1)
    lane_head = lax.broadcasted_iota(jnp.int32, (1, SB_WIDTH), 1) // SB_HEAD_DIM
    head_mask = [jnp.where(lane_head == h, 1.0, 0.0).astype(BF16) for h in range(SB_HEADS)]
    qs = q_ref[...] * (SB_HEAD_DIM ** -0.5)
    q_heads = [qs * head_mask[h] for h in range(SB_HEADS)]

    def block(j, accs, out, diagonal):
        start = pl.multiple_of(j * t, t)
        kb = k_ref[pl.ds(start, t), :]
        vb = v_ref[pl.ds(start, t), :]
        new_accs = []
        for h in range(SB_HEADS):
            z = lax.dot_general(q_heads[h], kb, (((1,), (1,)), ((), ())), preferred_element_type=F32)
            log1m = jnp.minimum(-z, 0.0) - jnp.log(1.0 + jnp.exp(-jnp.abs(z)))
            if diagonal:
                mask = col < row
                log1m_m = jnp.where(mask, log1m, 0.0)
            else:
                log1m_m = log1m
            rev = jnp.dot(log1m_m.astype(BF16), upper, preferred_element_type=F32)
            w = jnp.exp(log1m + z + rev + accs[h])
            if diagonal:
                w = jnp.where(mask, w, 0.0)
            out = out + jnp.dot(w.astype(BF16), vb * head_mask[h], preferred_element_type=F32)
            new_accs.append(accs[h] + rev[:, 0:1] + log1m_m[:, 0:1])
        return tuple(new_accs), out

    accs0 = tuple(jnp.zeros((t, 1), F32) for _ in range(SB_HEADS))
    out0 = jnp.zeros((t, SB_WIDTH), F32)
    accs0, out0 = block(i, accs0, out0, True)

    def cond(carry):
        j, accs, _ = carry
        worst = functools.reduce(jnp.maximum, accs)
        return jnp.logical_and(j >= 0, jnp.max(worst) > SB_SKIP_LOG)

    def body(carry):
        j, accs, out = carry
        accs, out = block(j, accs, out, False)
        return j - 1, accs, out

    _, _, out = lax.while_loop(cond, body, (i - 1, accs0, out0))
    o_ref[...] = out


def _sb_attention(qkv, batch, upper):
    n = qkv.shape[0]
    s = n // batch
    t = SB_BLOCK
    nq = s // t
    return pl.pallas_call(
        _sb_attn_kernel,
        grid=(batch, nq),
        in_specs=[
            pl.BlockSpec((t, SB_WIDTH), lambda b, i: (b * nq + i, 0)),
            pl.BlockSpec((s, SB_WIDTH), lambda b, i: (b, 1)),
            pl.BlockSpec((s, SB_WIDTH), lambda b, i: (b, 2)),
            pl.BlockSpec((t, t), lambda b, i: (0, 0)),
        ],
        out_specs=pl.BlockSpec((t, SB_WIDTH), lambda b, i: (b * nq + i, 0)),
        out_shape=jax.ShapeDtypeStruct((n, SB_WIDTH), F32),
        compiler_params=_params("parallel", "arbitrary"),
        name="sb_attention",
    )(qkv, qkv, qkv, upper)


def _mixer_kernel(rest_ref, ysb_ref, x_ref, convw_ref, convb_ref, dtb_ref, alog_ref, dexp_ref,
                  normw_ref, scw_ref, wout_ref, tri_ref, o_ref,
                  xbc_ext, sc_ext, state_ref):
    t = rest_ref.shape[0]
    sblk = pl.program_id(1)
    halo = 8

    @pl.when(sblk == 0)
    def _():
        xbc_ext[0:halo, :] = jnp.zeros((halo, SSM_CONV_DIM), F32)
        sc_ext[0:halo, :] = jnp.zeros((halo, SC_WIDTH), F32)
        state_ref[...] = jnp.zeros_like(state_ref)

    o_z = 0
    o_xbc = SSM_WIDTH
    o_dt = o_xbc + SSM_CONV_DIM
    o_sc = o_dt + DT_PAD
    z = rest_ref[:, o_z:o_z + SSM_WIDTH]
    dt_raw = rest_ref[:, o_dt:o_dt + DT_PAD]
    sc_b = rest_ref[:, o_sc:o_sc + SC_WIDTH]
    sc_c = rest_ref[:, o_sc + SC_WIDTH:o_sc + 2 * SC_WIDTH]
    sc_v = rest_ref[:, o_sc + 2 * SC_WIDTH:o_sc + 3 * SC_WIDTH]

    xbc_ext[halo:halo + t, :] = rest_ref[:, o_xbc:o_xbc + SSM_CONV_DIM]
    conv = convb_ref[...]
    for kk in range(SSM_CONV):
        shift = SSM_CONV - 1 - kk
        conv = conv + convw_ref[kk:kk + 1, :] * xbc_ext[halo - shift:halo - shift + t, :]
    tail = xbc_ext[t:t + halo, :]
    xbc_ext[0:halo, :] = tail
    xbc = _silu(conv)
    xs = xbc[:, :SSM_WIDTH]

    sc_ext[halo:halo + t, :] = sc_c * sc_v
    sconv = jnp.zeros((t, SC_WIDTH), F32)
    for kk in range(SC_CONV):
        shift = SC_CONV - 1 - kk
        sconv = sconv + scw_ref[kk:kk + 1, :] * sc_ext[halo - shift:halo - shift + t, :]
    sc_tail = sc_ext[t:t + halo, :]
    sc_ext[0:halo, :] = sc_tail
    y_sc = sc_b * sconv

    dt = _softplus(dt_raw + dtb_ref[...])
    a = dt * (-jnp.exp(alog_ref[...]))
    a_hi = a.astype(BF16)
    a_lo = (a - a_hi.astype(F32)).astype(BF16)
    tri = tri_ref[...]
    a_cs = (jnp.dot(tri, a_hi, preferred_element_type=F32)
            + jnp.dot(tri, a_lo, preferred_element_type=F32))
    a_cs_t = a_cs.T
    a_end = a_cs[t - 1:t, :]
    row = lax.broadcasted_iota(jnp.int32, (t, t), 0)
    col = lax.broadcasted_iota(jnp.int32, (t, t), 1)
    causal = row >= col

    def lanes_per_head(cols, h0, nh):
        return jnp.concatenate(
            [jnp.broadcast_to(cols[:, h:h + 1], (cols.shape[0], SSM_HEAD_DIM)) for h in range(h0, h0 + nh)],
            axis=1)

    hpg = SSM_HEADS // SSM_GROUPS
    gw = hpg * SSM_HEAD_DIM
    lane_head = lax.broadcasted_iota(jnp.int32, (t, gw), 1) // SSM_HEAD_DIM
    y_groups = []
    for g in range(SSM_GROUPS):
        b_g = xbc[:, SSM_WIDTH + g * SSM_STATE:SSM_WIDTH + (g + 1) * SSM_STATE]
        c_g = xbc[:, SSM_WIDTH + SSM_GROUPS * SSM_STATE + g * SSM_STATE:
                  SSM_WIDTH + SSM_GROUPS * SSM_STATE + (g + 1) * SSM_STATE]
        b_bf = b_g.astype(BF16)
        c_bf = c_g.astype(BF16)
        xs_g = xs[:, g * gw:(g + 1) * gw]
        dt_g = lanes_per_head(dt, g * hpg, hpg)
        acs_g = lanes_per_head(a_cs, g * hpg, hpg)
        aend_g = lanes_per_head(a_end, g * hpg, hpg)
        xd_g = xs_g * dt_g
        xd_bf = xd_g.astype(BF16)
        cb = lax.dot_general(c_bf, b_bf, (((1,), (1,)), ((), ())), preferred_element_type=F32)
        y_diag = jnp.zeros((t, gw), F32)
        for hh in range(hpg):
            h = g * hpg + hh
            seg = a_cs[:, h:h + 1] - a_cs_t[h:h + 1, :]
            lmat = jnp.where(causal, jnp.exp(jnp.where(causal, seg, 0.0)), 0.0)
            p_h = jnp.dot((cb * lmat).astype(BF16), xd_bf, preferred_element_type=F32)
            y_diag = jnp.where(lane_head == hh, p_h, y_diag)
        state = state_ref[g]
        y_off = jnp.dot(c_bf, state.astype(BF16), preferred_element_type=F32) * jnp.exp(acs_g)
        xdd = (xd_g * jnp.exp(aend_g - acs_g)).astype(BF16)
        contrib = lax.dot_general(b_bf, xdd, (((0,), (0,)), ((), ())), preferred_element_type=F32)
        state_ref[g] = state * jnp.exp(aend_g) + contrib
        y_groups.append(y_diag + y_off + dexp_ref[:, g * gw:(g + 1) * gw] * xs_g)
    y = jnp.concatenate(y_groups, axis=1) * _silu(z)
    y_ssm = _rmsnorm(y, normw_ref[...])

    o0 = SB_WIDTH
    o1 = o0 + SSM_WIDTH
    out = x_ref[...]
    out = out + jnp.dot(ysb_ref[...].astype(BF16), wout_ref[0:o0, :], preferred_element_type=F32)
    out = out + jnp.dot(y_ssm.astype(BF16), wout_ref[o0:o1, :], preferred_element_type=F32)
    out = out + jnp.dot(y_sc.astype(BF16), wout_ref[o1:o1 + SC_WIDTH, :], preferred_element_type=F32)
    o_ref[...] = out


def _mixer(rest, ysb, x2, batch, conv_w, conv_b, dt_bias, a_log, d_exp, norm_w, sc_w, w_out, tri):
    n = x2.shape[0]
    t = MIX_BLOCK
    spb = n // batch // t
    row_map = lambda b, s: (b * spb + s, 0)
    const = lambda b, s: (0, 0)
    return pl.pallas_call(
        _mixer_kernel,
        grid=(batch, spb),
        in_specs=[
            pl.BlockSpec((t, REST_WIDTH), row_map),
            pl.BlockSpec((t, SB_WIDTH), row_map),
            pl.BlockSpec((t, D_MODEL), row_map),
            pl.BlockSpec((SSM_CONV, SSM_CONV_DIM), const),
            pl.BlockSpec((1, SSM_CONV_DIM), const),
            pl.BlockSpec((1, DT_PAD), const),
            pl.BlockSpec((1, DT_PAD), const),
            pl.BlockSpec((1, SSM_WIDTH), const),
            pl.BlockSpec((1, SSM_WIDTH), const),
            pl.BlockSpec((SC_CONV, SC_WIDTH), const),
            pl.BlockSpec((D_MODEL, D_MODEL), const),
            pl.BlockSpec((t, t), const),
        ],
        out_specs=pl.BlockSpec((t, D_MODEL), row_map),
        out_shape=jax.ShapeDtypeStruct((n, D_MODEL), F32),
        scratch_shapes=[
            pltpu.VMEM((t + 8, SSM_CONV_DIM), F32),
            pltpu.VMEM((t + 8, SC_WIDTH), F32),
            pltpu.VMEM((SSM_GROUPS, SSM_STATE, (SSM_HEADS // SSM_GROUPS) * SSM_HEAD_DIM), F32),
        ],
        compiler_params=_params("arbitrary", "arbitrary"),
        name="mixer",
    )(rest, ysb, x2, conv_w, conv_b, dt_bias, a_log, d_exp, norm_w, sc_w, w_out, tri)


SUBLANES = 8


def _sort16_pairs():
    def merge(lo, hi, r):
        step = r * 2
        if step < hi - lo:
            yield from merge(lo, hi, step)
            yield from merge(lo + r, hi, step)
            for i in range(lo + r, hi - r, step):
                yield (i, i + r)
        else:
            yield (lo, lo + r)

    def sort(lo, hi):
        if hi - lo >= 1:
            mid = lo + (hi - lo) // 2
            yield from sort(lo, mid)
            yield from sort(mid + 1, hi)
            yield from merge(lo, hi, 1)

    return tuple(sort(0, PEER_TOPK - 1))


_SORT16 = _sort16_pairs()


def _exchange(v, i, j):
    v[i], v[j] = jnp.maximum(v[i], v[j]), jnp.minimum(v[i], v[j])


def _merge_sublane_lists(v):
    for shift in (4, 2, 1):
        r = [pltpu.roll(x, shift, axis=0) for x in v]
        v = [jnp.maximum(v[i], r[PEER_TOPK - 1 - i]) for i in range(PEER_TOPK)]
        d = PEER_TOPK // 2
        while d >= 1:
            for i in range(PEER_TOPK):
                if not i & d:
                    _exchange(v, i, i + d)
            d //= 2
    return v


def _top16_keys(scores):
    v = [scores[k * SUBLANES:(k + 1) * SUBLANES, :] for k in range(PEER_NKEYS // SUBLANES)]
    for i, j in _SORT16:
        _exchange(v, i, j)
    return _merge_sublane_lists(v)


def _top16_pairs(top0, top1):
    t = top0[0].shape[1]
    sub = lax.broadcasted_iota(jnp.int32, (SUBLANES, t), 0)
    lengths = (16, 8, 5, 4, 12, 4, 1, 0)
    length = jnp.zeros((SUBLANES, t), jnp.int32)
    for s_, n_ in enumerate(lengths):
        length = jnp.where(sub == s_, n_, length)
    row_part = jnp.where(sub == 0, top0[0], jnp.where(sub == 1, top0[1],
                                                     jnp.where(sub == 2, top0[2], top0[3])))
    col_part = jnp.where(sub == 4, top1[0], jnp.where(sub == 5, top1[1], top1[2]))
    lists = []
    for r in range(PEER_TOPK):
        a = jnp.where(sub < 4, row_part, top0[min(4 + r, PEER_TOPK - 1)])
        b = jnp.where(sub < 4, top1[r], col_part)
        lists.append(jnp.where(r < length, a + b, NEG_INF))
    return _merge_sublane_lists(lists)


def _split_bf16(x):
    hi = x.astype(BF16)
    return hi, (x - hi.astype(F32)).astype(BF16)


def _dot3(a_hi, a_lo, b_hi, b_lo):
    return (jnp.dot(a_hi, b_hi, preferred_element_type=F32)
            + jnp.dot(a_hi, b_lo, preferred_element_type=F32)
            + jnp.dot(a_lo, b_hi, preferred_element_type=F32))


def _route_kernel(x_ref, nw_ref, wqt_hi_ref, wqt_lo_ref, keys_hi_ref, keys_lo_ref,
                  ht_ref, n0_ref, w0_ref, r1_ref, e1_ref):
    h = _rmsnorm(x_ref[...], nw_ref[...])
    ht_hi, ht_lo = _split_bf16(h.T)
    ht_ref[...] = ht_hi
    qt = _dot3(wqt_hi_ref[...], wqt_lo_ref[...], ht_hi, ht_lo)
    half = PEER_KEY_DIM // 2
    for hh in range(PEER_HEADS):
        sc = []
        for side in range(2):
            lo = hh * PEER_KEY_DIM + side * half
            q_hi, q_lo = _split_bf16(qt[lo:lo + half, :])
            sc.append(_dot3(keys_hi_ref[side], keys_lo_ref[side], q_hi, q_lo))
        for l0 in range(0, sc[0].shape[1], LANES):
            ln = slice(l0, l0 + LANES)
            _route_lane_tile(sc[0][:, ln], sc[1][:, ln], hh, ln, n0_ref, w0_ref, r1_ref, e1_ref)


def _route_lane_tile(s0, s1, hh, ln, n0_ref, w0_ref, r1_ref, e1_ref):
    top0 = _top16_keys(s0)
    top1 = _top16_keys(s1)
    best = _top16_pairs(top0, top1)
    z = jnp.exp(best[0] - best[0])
    for r in range(1, PEER_TOPK):
        z = z + jnp.exp(best[r] - best[0])
    tau = best[PEER_TOPK - 1]
    scale0 = INV_SQRT2 / z
    r1_rows, e1_rows = [], []
    for k in range(PEER_NKEYS // SUBLANES):
        rows = slice(k * SUBLANES, (k + 1) * SUBLANES)
        s0k = s0[rows, :]
        s1k = s1[rows, :]
        n0 = jnp.zeros_like(s0k)
        r1 = jnp.zeros_like(s1k)
        for j in range(PEER_TOPK):
            n0 = n0 + jnp.where(s0k + top1[j] >= tau, 1.0, 0.0)
            r1 = r1 + jnp.where(top1[j] > s1k, 1.0, 0.0)
        n0_ref[hh, rows, ln] = n0
        w0_ref[hh, rows, ln] = jnp.exp(s0k - top0[0]) * scale0
        r1_rows.append(r1)
        e1_rows.append(jnp.exp(s1k - top1[0]))
    r1_ref[hh, :, ln] = jnp.concatenate(r1_rows, axis=0).astype(BF16)
    e1_ref[hh, :, ln] = jnp.concatenate(e1_rows, axis=0).astype(BF16)


def _route(x2, norm_w, wq_t, keys):
    n = x2.shape[0]
    t = ROUTE_BLOCK
    hk = (PEER_HEADS, PEER_NKEYS, n)
    hk_spec = pl.BlockSpec((PEER_HEADS, PEER_NKEYS, t), lambda i: (0, 0, i))
    wq_spec = pl.BlockSpec((PEER_HEADS * PEER_KEY_DIM, D_MODEL), lambda i: (0, 0))
    keys_spec = pl.BlockSpec((2, PEER_NKEYS, PEER_KEY_DIM // 2), lambda i: (0, 0, 0))
    wq_hi, wq_lo = _split_bf16(wq_t)
    keys_hi, keys_lo = _split_bf16(keys)
    return pl.pallas_call(
        _route_kernel,
        grid=(n // t,),
        in_specs=[
            pl.BlockSpec((t, D_MODEL), lambda i: (i, 0)),
            pl.BlockSpec((1, D_MODEL), lambda i: (0, 0)),
            wq_spec, wq_spec, keys_spec, keys_spec,
        ],
        out_specs=[
            pl.BlockSpec((D_MODEL, t), lambda i: (0, i)),
            hk_spec, hk_spec, hk_spec, hk_spec,
        ],
        out_shape=[
            jax.ShapeDtypeStruct((D_MODEL, n), BF16),
            jax.ShapeDtypeStruct(hk, F32), jax.ShapeDtypeStruct(hk, F32),
            jax.ShapeDtypeStruct(hk, BF16), jax.ShapeDtypeStruct(hk, BF16),
        ],
        compiler_params=_params("parallel"),
        name="peer_route",
    )(x2, norm_w, wq_hi, wq_lo, keys_hi, keys_lo)


def _expert_kernel(ht_ref, n0_ref, w0_ref, r1_ref, e1_ref, u_ref, vt_ref, x_ref, fw_ref,
                   o_ref, acc_ref, *, final_norm):
    e = pl.program_id(1)
    eb = u_ref.shape[0]
    t = ht_ref.shape[1]

    @pl.when(e == 0)
    def _():
        acc_ref[...] = jnp.zeros_like(acc_ref)

    y_all = jnp.dot(u_ref[...], ht_ref[...], preferred_element_type=F32)
    rows = []
    for sub in range(eb // PEER_NKEYS):
        y = y_all[sub * PEER_NKEYS:(sub + 1) * PEER_NKEYS, :]
        gate = jnp.zeros((PEER_NKEYS, t), BF16)
        for hh in range(PEER_HEADS):
            n0 = jnp.broadcast_to(n0_ref[hh, sub:sub + 1, :].astype(BF16), (PEER_NKEYS, t))
            w0 = jnp.broadcast_to(w0_ref[hh, sub:sub + 1, :].astype(BF16), (PEER_NKEYS, t))
            gate = gate + jnp.where(r1_ref[hh] < n0, e1_ref[hh], 0.0) * w0
        rows.append(gate * (y * (1.0 + lax.erf(y))).astype(BF16))
    weighted = jnp.concatenate(rows, axis=0)
    acc_ref[...] += jnp.dot(vt_ref[...], weighted, preferred_element_type=F32)

    @pl.when(e == pl.num_programs(1) - 1)
    def _():
        out = x_ref[...] + acc_ref[...].T
        if final_norm:
            out = _rmsnorm(out, fw_ref[...])
        o_ref[...] = out


def _experts(ht, n0, w0, r1, e1, u_bf, vt_bf, x2, final_w, final_norm):
    n = x2.shape[0]
    t = PEER_TOKENS
    eb = PEER_EXPERTS_PER_STEP
    ne = u_bf.shape[0] // eb
    hk_spec = pl.BlockSpec((PEER_HEADS, PEER_NKEYS, t), lambda i, e: (0, 0, i))
    ha_spec = pl.BlockSpec((PEER_HEADS, eb // PEER_NKEYS, t), lambda i, e: (0, e, i))
    return pl.pallas_call(
        functools.partial(_expert_kernel, final_norm=final_norm),
        grid=(n // t, ne),
        in_specs=[
            pl.BlockSpec((D_MODEL, t), lambda i, e: (0, i)),
            ha_spec, ha_spec, hk_spec, hk_spec,
            pl.BlockSpec((eb, D_MODEL), lambda i, e: (e, 0)),
            pl.BlockSpec((D_MODEL, eb), lambda i, e: (0, e)),
            pl.BlockSpec((t, D_MODEL), lambda i, e: (i, 0)),
            pl.BlockSpec((1, D_MODEL), lambda i, e: (0, 0)),
        ],
        out_specs=pl.BlockSpec((t, D_MODEL), lambda i, e: (i, 0)),
        out_shape=jax.ShapeDtypeStruct((n, D_MODEL), F32),
        scratch_shapes=[pltpu.VMEM((D_MODEL, t), F32)],
        compiler_params=_params("parallel", "arbitrary"),
        name="peer_experts",
    )(ht, n0, w0, r1, e1, u_bf, vt_bf, x2, final_w)


def _pad_lanes(v, width):
    return jnp.pad(v, ((0, 0), (0, width - v.shape[1])))


def kernel(x, norm1_w, w_in, ssm_conv_w, ssm_conv_b, ssm_dt_bias, ssm_A_log, ssm_D, ssm_norm_w,
           sc_conv_w, w_out, norm2_w, peer_w_q, peer_sub_keys, peer_u, peer_v, final_norm_w):
    batch, seq, d = x.shape
    depth = w_in.shape[0]
    n = batch * seq
    x2 = x.reshape(n, d)

    cuts = [0]
    for width in (SB_WIDTH, SB_WIDTH, SB_WIDTH, SSM_WIDTH, SSM_CONV_DIM, SSM_HEADS,
                  SC_WIDTH, SC_WIDTH, SC_WIDTH):
        cuts.append(cuts[-1] + width)
    ident = jnp.arange(SB_BLOCK)
    upper = (ident[:, None] > ident[None, :]).astype(BF16)
    identm = jnp.arange(MIX_BLOCK)
    tri = (identm[None, :] <= identm[:, None]).astype(BF16)

    for layer in range(depth):
        w = w_in[layer]
        w_cat = jnp.concatenate([
            w[:, cuts[3]:cuts[5]],
            _pad_lanes(w[:, cuts[5]:cuts[6]], DT_PAD),
            w[:, cuts[6]:cuts[9]],
            w[:, cuts[0]:cuts[3]],
        ], axis=1).astype(BF16)
        rest, qkv = _in_proj(x2, norm1_w[layer][None, :], w_cat)

        ysb = _sb_attention(qkv, batch, upper)

        x2 = _mixer(
            rest, ysb, x2, batch,
            ssm_conv_w[layer], ssm_conv_b[layer][None, :],
            _pad_lanes(ssm_dt_bias[layer][None, :], DT_PAD),
            _pad_lanes(ssm_A_log[layer][None, :], DT_PAD),
            jnp.repeat(ssm_D[layer], SSM_HEAD_DIM)[None, :],
            ssm_norm_w[layer][None, :], sc_conv_w[layer],
            w_out[layer].astype(BF16), tri)

        ht, n0, w0, r1, e1 = _route(
            x2, norm2_w[layer][None, :], peer_w_q[layer].T, peer_sub_keys[layer])
        x2 = _experts(ht, n0, w0, r1, e1,
                      (peer_u[layer] * INV_SQRT2).astype(BF16), peer_v[layer].T.astype(BF16),
                      x2, final_norm_w[None, :], layer == depth - 1)
    return x2.reshape(batch, seq, d)
```

```python
import functools
import math

import jax
import jax.numpy as jnp
from jax import lax
from jax.experimental import pallas as pl
from jax.experimental.pallas import tpu as pltpu

F32 = jnp.float32
BF16 = jnp.bfloat16

EPS = 1e-6
D_MODEL = 1024
SB_HEADS = 4
SB_HEAD_DIM = 64
SB_WIDTH = SB_HEADS * SB_HEAD_DIM
SSM_HEADS = 8
SSM_HEAD_DIM = 64
SSM_WIDTH = SSM_HEADS * SSM_HEAD_DIM
SSM_GROUPS = 2
SSM_STATE = 64
SSM_CONV = 4
SSM_CONV_DIM = SSM_WIDTH + 2 * SSM_GROUPS * SSM_STATE
SC_WIDTH = 256
SC_CONV = 3
PEER_HEADS = 4
PEER_KEY_DIM = 256
PEER_NKEYS = 128
PEER_TOPK = 16

LANES = 128
DT_PAD = LANES
REST_WIDTH = SSM_WIDTH + SSM_CONV_DIM + DT_PAD + 3 * SC_WIDTH
QKV_WIDTH = 3 * SB_WIDTH
VMEM_LIMIT = 56 * 1024 * 1024

IN_PROJ_ROWS = 256
SB_BLOCK = 256
MIX_BLOCK = 256
ROUTE_BLOCK = 256
PEER_TOKENS = 512
PEER_EXPERTS_PER_STEP = 2048
SB_SKIP_LOG = -120.0
INV_SQRT2 = 1.0 / math.sqrt(2.0)
NEG_INF = float("-inf")


def _rmsnorm(x, w):
    return x * lax.rsqrt(jnp.mean(x * x, axis=-1, keepdims=True) + EPS) * w


def _softplus(x):
    return jnp.maximum(x, 0.0) + jnp.log(1.0 + jnp.exp(-jnp.abs(x)))


def _silu(x):
    return x * (1.0 / (1.0 + jnp.exp(-x)))


def _params(*semantics):
    return pltpu.CompilerParams(dimension_semantics=semantics, vmem_limit_bytes=VMEM_LIMIT)


def _in_proj_kernel(x_ref, nw_ref, w_ref, rest_ref, qkv_ref):
    h = _rmsnorm(x_ref[...], nw_ref[...]).astype(BF16)
    p = jnp.dot(h, w_ref[...], preferred_element_type=F32)
    rest_ref[...] = p[:, :REST_WIDTH]
    qkv_ref[...] = p[:, REST_WIDTH:].astype(BF16)


def _in_proj(x2, norm_w, w_cat):
    n = x2.shape[0]
    tm = IN_PROJ_ROWS
    width = w_cat.shape[1]
    return pl.pallas_call(
        _in_proj_kernel,
        grid=(n // tm,),
        in_specs=[
            pl.BlockSpec((tm, D_MODEL), lambda i: (i, 0)),
            pl.BlockSpec((1, D_MODEL), lambda i: (0, 0)),
            pl.BlockSpec((D_MODEL, width), lambda i: (0, 0)),
        ],
        out_specs=[
            pl.BlockSpec((tm, REST_WIDTH), lambda i: (i, 0)),
            pl.BlockSpec((tm, QKV_WIDTH), lambda i: (i, 0)),
        ],
        out_shape=[
            jax.ShapeDtypeStruct((n, REST_WIDTH), F32),
            jax.ShapeDtypeStruct((n, QKV_WIDTH), BF16),
        ],
        compiler_params=_params("parallel"),
        name="in_proj",
    )(x2, norm_w, w_cat)


def _sb_attn_kernel(q_ref, k_ref, v_ref, upper_ref, o_ref):
    t = q_ref.shape[0]
    i = pl.program_id(1)
    upper = upper_ref[...]
    row = lax.broadcasted_iota(jnp.int32, (t, t), 0)
    col = lax.broadcasted_iota(jnp.int32, (t, t), 1)
    lane_head = lax.broadcasted_iota(jnp.int32, (1, SB_WIDTH), 1) // SB_HEAD_DIM
    head_mask = [jnp.where(lane_head == h, 1.0, 0.0).astype(BF16) for h in range(SB_HEADS)]
    qs = q_ref[...] * (SB_HEAD_DIM ** -0.5)
    q_heads = [qs * head_mask[h] for h in range(SB_HEADS)]

    def block(j, accs, out, diagonal):
        start = pl.multiple_of(j * t, t)
        kb = k_ref[pl.ds(start, t), :]
        vb = v_ref[pl.ds(start, t), :]
        new_accs = []
        for h in range(SB_HEADS):
            z = lax.dot_general(q_heads[h], kb, (((1,), (1,)), ((), ())), preferred_element_type=F32)
            log1m = jnp.minimum(-z, 0.0) - jnp.log(1.0 + jnp.exp(-jnp.abs(z)))
            if diagonal:
                mask = col < row
                log1m_m = jnp.where(mask, log1m, 0.0)
            else:
                log1m_m = log1m
            rev = jnp.dot(log1m_m.astype(BF16), upper, preferred_element_type=F32)
            w = jnp.exp(log1m + z + rev + accs[h])
            if diagonal:
                w = jnp.where(mask, w, 0.0)
            out = out + jnp.dot(w.astype(BF16), vb * head_mask[h], preferred_element_type=F32)
            new_accs.append(accs[h] + rev[:, 0:1] + log1m_m[:, 0:1])
        return tuple(new_accs), out

    accs0 = tuple(jnp.zeros((t, 1), F32) for _ in range(SB_HEADS))
    out0 = jnp.zeros((t, SB_WIDTH), F32)
    accs0, out0 = block(i, accs0, out0, True)

    def cond(carry):
        j, accs, _ = carry
        worst = functools.reduce(jnp.maximum, accs)
        return jnp.logical_and(j >= 0, jnp.max(worst) > SB_SKIP_LOG)

    def body(carry):
        j, accs, out = carry
        accs, out = block(j, accs, out, False)
        return j - 1, accs, out

    _, _, out = lax.while_loop(cond, body, (i - 1, accs0, out0))
    o_ref[...] = out


def _sb_attention(qkv, batch, upper):
    n = qkv.shape[0]
    s = n // batch
    t = SB_BLOCK
    nq = s // t
    return pl.pallas_call(
        _sb_attn_kernel,
        grid=(batch, nq),
        in_specs=[
            pl.BlockSpec((t, SB_WIDTH), lambda b, i: (b * nq + i, 0)),
            pl.BlockSpec((s, SB_WIDTH), lambda b, i: (b, 1)),
            pl.BlockSpec((s, SB_WIDTH), lambda b, i: (b, 2)),
            pl.BlockSpec((t, t), lambda b, i: (0, 0)),
        ],
        out_specs=pl.BlockSpec((t, SB_WIDTH), lambda b, i: (b * nq + i, 0)),
        out_shape=jax.ShapeDtypeStruct((n, SB_WIDTH), F32),
        compiler_params=_params("parallel", "arbitrary"),
        name="sb_attention",
    )(qkv, qkv, qkv, upper)


def _mixer_kernel(rest_ref, ysb_ref, x_ref, convw_ref, convb_ref, dtb_ref, alog_ref, dexp_ref,
                  normw_ref, scw_ref, wout_ref, tri_ref, o_ref,
                  xbc_ext, sc_ext, state_ref):
    t = rest_ref.shape[0]
    sblk = pl.program_id(1)
    halo = 8

    @pl.when(sblk == 0)
    def _():
        xbc_ext[0:halo, :] = jnp.zeros((halo, SSM_CONV_DIM), F32)
        sc_ext[0:halo, :] = jnp.zeros((halo, SC_WIDTH), F32)
        state_ref[...] = jnp.zeros_like(state_ref)

    o_z = 0
    o_xbc = SSM_WIDTH
    o_dt = o_xbc + SSM_CONV_DIM
    o_sc = o_dt + DT_PAD
    z = rest_ref[:, o_z:o_z + SSM_WIDTH]
    dt_raw = rest_ref[:, o_dt:o_dt + DT_PAD]
    sc_b = rest_ref[:, o_sc:o_sc + SC_WIDTH]
    sc_c = rest_ref[:, o_sc + SC_WIDTH:o_sc + 2 * SC_WIDTH]
    sc_v = rest_ref[:, o_sc + 2 * SC_WIDTH:o_sc + 3 * SC_WIDTH]

    xbc_ext[halo:halo + t, :] = rest_ref[:, o_xbc:o_xbc + SSM_CONV_DIM]
    conv = convb_ref[...]
    for kk in range(SSM_CONV):
        shift = SSM_CONV - 1 - kk
        conv = conv + convw_ref[kk:kk + 1, :] * xbc_ext[halo - shift:halo - shift + t, :]
    tail = xbc_ext[t:t + halo, :]
    xbc_ext[0:halo, :] = tail
    xbc = _silu(conv)
    xs = xbc[:, :SSM_WIDTH]

    sc_ext[halo:halo + t, :] = sc_c * sc_v
    sconv = jnp.zeros((t, SC_WIDTH), F32)
    for kk in range(SC_CONV):
        shift = SC_CONV - 1 - kk
        sconv = sconv + scw_ref[kk:kk + 1, :] * sc_ext[halo - shift:halo - shift + t, :]
    sc_tail = sc_ext[t:t + halo, :]
    sc_ext[0:halo, :] = sc_tail
    y_sc = sc_b * sconv

    dt = _softplus(dt_raw + dtb_ref[...])
    a = dt * (-jnp.exp(alog_ref[...]))
    a_hi = a.astype(BF16)
    a_lo = (a - a_hi.astype(F32)).astype(BF16)
    tri = tri_ref[...]
    a_cs = (jnp.dot(tri, a_hi, preferred_element_type=F32)
            + jnp.dot(tri, a_lo, preferred_element_type=F32))
    a_cs_t = a_cs.T
    a_end = a_cs[t - 1:t, :]
    row = lax.broadcasted_iota(jnp.int32, (t, t), 0)
    col = lax.broadcasted_iota(jnp.int32, (t, t), 1)
    causal = row >= col

    def lanes_per_head(cols, h0, nh):
        return jnp.concatenate(
            [jnp.broadcast_to(cols[:, h:h + 1], (cols.shape[0], SSM_HEAD_DIM)) for h in range(h0, h0 + nh)],
            axis=1)

    hpg = SSM_HEADS // SSM_GROUPS
    gw = hpg * SSM_HEAD_DIM
    lane_head = lax.broadcasted_iota(jnp.int32, (t, gw), 1) // SSM_HEAD_DIM
    y_groups = []
    for g in range(SSM_GROUPS):
        b_g = xbc[:, SSM_WIDTH + g * SSM_STATE:SSM_WIDTH + (g + 1) * SSM_STATE]
        c_g = xbc[:, SSM_WIDTH + SSM_GROUPS * SSM_STATE + g * SSM_STATE:
                  SSM_WIDTH + SSM_GROUPS * SSM_STATE + (g + 1) * SSM_STATE]
        b_bf = b_g.astype(BF16)
        c_bf = c_g.astype(BF16)
        xs_g = xs[:, g * gw:(g + 1) * gw]
        dt_g = lanes_per_head(dt, g * hpg, hpg)
        acs_g = lanes_per_head(a_cs, g * hpg, hpg)
        aend_g = lanes_per_head(a_end, g * hpg, hpg)
        xd_g = xs_g * dt_g
        xd_bf = xd_g.astype(BF16)
        cb = lax.dot_general(c_bf, b_bf, (((1,), (1,)), ((), ())), preferred_element_type=F32)
        y_diag = jnp.zeros((t, gw), F32)
        for hh in range(hpg):
            h = g * hpg + hh
            seg = a_cs[:, h:h + 1] - a_cs_t[h:h + 1, :]
            lmat = jnp.where(causal, jnp.exp(jnp.where(causal, seg, 0.0)), 0.0)
            p_h = jnp.dot((cb * lmat).astype(BF16), xd_bf, preferred_element_type=F32)
            y_diag = jnp.where(lane_head == hh, p_h, y_diag)
        state = state_ref[g]
        y_off = jnp.dot(c_bf, state.astype(BF16), preferred_element_type=F32) * jnp.exp(acs_g)
        xdd = (xd_g * jnp.exp(aend_g - acs_g)).astype(BF16)
        contrib = lax.dot_general(b_bf, xdd, (((0,), (0,)), ((), ())), preferred_element_type=F32)
        state_ref[g] = state * jnp.exp(aend_g) + contrib
        y_groups.append(y_diag + y_off + dexp_ref[:, g * gw:(g + 1) * gw] * xs_g)
    y = jnp.concatenate(y_groups, axis=1) * _silu(z)
    y_ssm = _rmsnorm(y, normw_ref[...])

    o0 = SB_WIDTH
    o1 = o0 + SSM_WIDTH
    out = x_ref[...]
    out = out + jnp.dot(ysb_ref[...].astype(BF16), wout_ref[0:o0, :], preferred_element_type=F32)
    out = out + jnp.dot(y_ssm.astype(BF16), wout_ref[o0:o1, :], preferred_element_type=F32)
    out = out + jnp.dot(y_sc.astype(BF16), wout_ref[o1:o1 + SC_WIDTH, :], preferred_element_type=F32)
    o_ref[...] = out


def _mixer(rest, ysb, x2, batch, conv_w, conv_b, dt_bias, a_log, d_exp, norm_w, sc_w, w_out, tri):
    n = x2.shape[0]
    t = MIX_BLOCK
    spb = n // batch // t
    row_map = lambda b, s: (b * spb + s, 0)
    const = lambda b, s: (0, 0)
    return pl.pallas_call(
        _mixer_kernel,
        grid=(batch, spb),
        in_specs=[
            pl.BlockSpec((t, REST_WIDTH), row_map),
            pl.BlockSpec((t, SB_WIDTH), row_map),
            pl.BlockSpec((t, D_MODEL), row_map),
            pl.BlockSpec((SSM_CONV, SSM_CONV_DIM), const),
            pl.BlockSpec((1, SSM_CONV_DIM), const),
            pl.BlockSpec((1, DT_PAD), const),
            pl.BlockSpec((1, DT_PAD), const),
            pl.BlockSpec((1, SSM_WIDTH), const),
            pl.BlockSpec((1, SSM_WIDTH), const),
            pl.BlockSpec((SC_CONV, SC_WIDTH), const),
            pl.BlockSpec((D_MODEL, D_MODEL), const),
            pl.BlockSpec((t, t), const),
        ],
        out_specs=pl.BlockSpec((t, D_MODEL), row_map),
        out_shape=jax.ShapeDtypeStruct((n, D_MODEL), F32),
        scratch_shapes=[
            pltpu.VMEM((t + 8, SSM_CONV_DIM), F32),
            pltpu.VMEM((t + 8, SC_WIDTH), F32),
            pltpu.VMEM((SSM_GROUPS, SSM_STATE, (SSM_HEADS // SSM_GROUPS) * SSM_HEAD_DIM), F32),
        ],
        compiler_params=_params("arbitrary", "arbitrary"),
        name="mixer",
    )(rest, ysb, x2, conv_w, conv_b, dt_bias, a_log, d_exp, norm_w, sc_w, w_out, tri)


SUBLANES = 8


def _sort16_pairs():
    def merge(lo, hi, r):
        step = r * 2
        if step < hi - lo:
            yield from merge(lo, hi, step)
            yield from merge(lo + r, hi, step)
            for i in range(lo + r, hi - r, step):
                yield (i, i + r)
        else:
            yield (lo, lo + r)

    def sort(lo, hi):
        if hi - lo >= 1:
            mid = lo + (hi - lo) // 2
            yield from sort(lo, mid)
            yield from sort(mid + 1, hi)
            yield from merge(lo, hi, 1)

    return tuple(sort(0, PEER_TOPK - 1))


_SORT16 = _sort16_pairs()


def _exchange(v, i, j):
    v[i], v[j] = jnp.maximum(v[i], v[j]), jnp.minimum(v[i], v[j])


def _merge_sublane_lists(v):
    for shift in (4, 2, 1):
        r = [pltpu.roll(x, shift, axis=0) for x in v]
        v = [jnp.maximum(v[i], r[PEER_TOPK - 1 - i]) for i in range(PEER_TOPK)]
        d = PEER_TOPK // 2
        while d >= 1:
            for i in range(PEER_TOPK):
                if not i & d:
                    _exchange(v, i, i + d)
            d //= 2
    return v


def _top16_keys(scores):
    v = [scores[k * SUBLANES:(k + 1) * SUBLANES, :] for k in range(PEER_NKEYS // SUBLANES)]
    for i, j in _SORT16:
        _exchange(v, i, j)
    return _merge_sublane_lists(v)


def _top16_pairs(top0, top1):
    t = top0[0].shape[1]
    sub = lax.broadcasted_iota(jnp.int32, (SUBLANES, t), 0)
    lengths = (16, 8, 5, 4, 12, 4, 1, 0)
    length = jnp.zeros((SUBLANES, t), jnp.int32)
    for s_, n_ in enumerate(lengths):
        length = jnp.where(sub == s_, n_, length)
    row_part = jnp.where(sub == 0, top0[0], jnp.where(sub == 1, top0[1],
                                                     jnp.where(sub == 2, top0[2], top0[3])))
    col_part = jnp.where(sub == 4, top1[0], jnp.where(sub == 5, top1[1], top1[2]))
    lists = []
    for r in range(PEER_TOPK):
        a = jnp.where(sub < 4, row_part, top0[min(4 + r, PEER_TOPK - 1)])
        b = jnp.where(sub < 4, top1[r], col_part)
        lists.append(jnp.where(r < length, a + b, NEG_INF))
    return _merge_sublane_lists(lists)


def _split_bf16(x):
    hi = x.astype(BF16)
    return hi, (x - hi.astype(F32)).astype(BF16)


def _dot3(a_hi, a_lo, b_hi, b_lo):
    return (jnp.dot(a_hi, b_hi, preferred_element_type=F32)
            + jnp.dot(a_hi, b_lo, preferred_element_type=F32)
            + jnp.dot(a_lo, b_hi, preferred_element_type=F32))


def _route_kernel(x_ref, nw_ref, wqt_hi_ref, wqt_lo_ref, keys_hi_ref, keys_lo_ref,
                  ht_ref, n0_ref, w0_ref, r1_ref, e1_ref):
    h = _rmsnorm(x_ref[...], nw_ref[...])
    ht_hi, ht_lo = _split_bf16(h.T)
    ht_ref[...] = ht_hi
    qt = _dot3(wqt_hi_ref[...], wqt_lo_ref[...], ht_hi, ht_lo)
    half = PEER_KEY_DIM // 2
    for hh in range(PEER_HEADS):
        sc = []
        for side in range(2):
            lo = hh * PEER_KEY_DIM + side * half
            q_hi, q_lo = _split_bf16(qt[lo:lo + half, :])
            sc.append(_dot3(keys_hi_ref[side], keys_lo_ref[side], q_hi, q_lo))
        for l0 in range(0, sc[0].shape[1], LANES):
            ln = slice(l0, l0 + LANES)
            _route_lane_tile(sc[0][:, ln], sc[1][:, ln], hh, ln, n0_ref, w0_ref, r1_ref, e1_ref)


def _count_leading(t, pred):
    c8 = pred(t[7])
    c4 = pred(jnp.where(c8, t[11], t[3]))
    c2 = pred(jnp.where(c8, jnp.where(c4, t[13], t[9]), jnp.where(c4, t[5], t[1])))
    c1 = pred(jnp.where(c8,
                        jnp.where(c4, jnp.where(c2, t[14], t[12]), jnp.where(c2, t[10], t[8])),
                        jnp.where(c4, jnp.where(c2, t[6], t[4]), jnp.where(c2, t[2], t[0]))))
    count = (jnp.where(c8, 8.0, 0.0) + jnp.where(c4, 4.0, 0.0)
             + jnp.where(c2, 2.0, 0.0) + jnp.where(c1, 1.0, 0.0))
    return count + jnp.where(pred(t[15]), 1.0, 0.0)


def _route_lane_tile(s0, s1, hh, ln, n0_ref, w0_ref, r1_ref, e1_ref):
    top0 = _top16_keys(s0)
    top1 = _top16_keys(s1)
    best = _top16_pairs(top0, top1)
    z = jnp.exp(best[0] - best[0])
    for r in range(1, PEER_TOPK):
        z = z + jnp.exp(best[r] - best[0])
    tau = best[PEER_TOPK - 1]
    scale0 = INV_SQRT2 / z
    r1_rows, e1_rows = [], []
    for k in range(PEER_NKEYS // SUBLANES):
        rows = slice(k * SUBLANES, (k + 1) * SUBLANES)
        s0k = s0[rows, :]
        s1k = s1[rows, :]
        n0_ref[hh, rows, ln] = _count_leading(top1, lambda m: s0k + m >= tau)
        w0_ref[hh, rows, ln] = jnp.exp(s0k - top0[0]) * scale0
        r1_rows.append(_count_leading(top1, lambda m: m > s1k))
        e1_rows.append(jnp.exp(s1k - top1[0]))
    r1_ref[hh, :, ln] = jnp.concatenate(r1_rows, axis=0).astype(BF16)
    e1_ref[hh, :, ln] = jnp.concatenate(e1_rows, axis=0).astype(BF16)


def _route(x2, norm_w, wq_t, keys):
    n = x2.shape[0]
    t = ROUTE_BLOCK
    hk = (PEER_HEADS, PEER_NKEYS, n)
    hk_spec = pl.BlockSpec((PEER_HEADS, PEER_NKEYS, t), lambda i: (0, 0, i))
    wq_spec = pl.BlockSpec((PEER_HEADS * PEER_KEY_DIM, D_MODEL), lambda i: (0, 0))
    keys_spec = pl.BlockSpec((2, PEER_NKEYS, PEER_KEY_DIM // 2), lambda i: (0, 0, 0))
    wq_hi, wq_lo = _split_bf16(wq_t)
    keys_hi, keys_lo = _split_bf16(keys)
    return pl.pallas_call(
        _route_kernel,
        grid=(n // t,),
        in_specs=[
            pl.BlockSpec((t, D_MODEL), lambda i: (i, 0)),
            pl.BlockSpec((1, D_MODEL), lambda i: (0, 0)),
            wq_spec, wq_spec, keys_spec, keys_spec,
        ],
        out_specs=[
            pl.BlockSpec((D_MODEL, t), lambda i: (0, i)),
            hk_spec, hk_spec, hk_spec, hk_spec,
        ],
        out_shape=[
            jax.ShapeDtypeStruct((D_MODEL, n), BF16),
            jax.ShapeDtypeStruct(hk, F32), jax.ShapeDtypeStruct(hk, F32),
            jax.ShapeDtypeStruct(hk, BF16), jax.ShapeDtypeStruct(hk, BF16),
        ],
        compiler_params=_params("parallel"),
        name="peer_route",
    )(x2, norm_w, wq_hi, wq_lo, keys_hi, keys_lo)


def _bf16_rows(row, n):
    packed = 2 * SUBLANES
    tile = jnp.broadcast_to(row, (packed, row.shape[1])).astype(BF16)
    return jnp.concatenate([tile] * (n // packed), axis=0)


def _expert_kernel(ht_ref, n0_ref, w0_ref, r1_ref, e1_ref, u_ref, vt_ref, x_ref, fw_ref,
                   o_ref, acc_ref, *, final_norm):
    e = pl.program_id(1)
    eb = u_ref.shape[0]
    t = ht_ref.shape[1]

    @pl.when(e == 0)
    def _():
        acc_ref[...] = jnp.zeros_like(acc_ref)

    def gated(y, sub):
        gate = jnp.zeros((PEER_NKEYS, t), BF16)
        for hh in range(PEER_HEADS):
            n0 = _bf16_rows(n0_ref[hh, sub:sub + 1, :], PEER_NKEYS)
            w0 = _bf16_rows(w0_ref[hh, sub:sub + 1, :], PEER_NKEYS)
            gate = gate + jnp.where(r1_ref[hh] < n0, e1_ref[hh], 0.0) * w0
        return gate * (y * (1.0 + lax.erf(y))).astype(BF16)

    half = eb // 2
    nsub = half // PEER_NKEYS
    ht = ht_ref[...]
    y_a = jnp.dot(u_ref[0:half, :], ht, preferred_element_type=F32)
    rows_a = [gated(y_a[s * PEER_NKEYS:(s + 1) * PEER_NKEYS, :], s) for s in range(nsub)]
    bits = pltpu.bitcast(rows_a[0][0:2 * SUBLANES, 0:LANES], jnp.uint32)
    sixteen = jnp.uint32(16)
    zero = pltpu.bitcast(lax.shift_right_logical(lax.shift_right_logical(bits, sixteen), sixteen), BF16)
    ht_b = ht + jnp.tile(zero, (ht.shape[0] // zero.shape[0], t // LANES))
    y_b = jnp.dot(u_ref[half:eb, :], ht_b, preferred_element_type=F32)
    rows_b = [gated(y_b[s * PEER_NKEYS:(s + 1) * PEER_NKEYS, :], nsub + s) for s in range(nsub)]
    weighted = jnp.concatenate(rows_a + rows_b, axis=0)
    acc_ref[...] += jnp.dot(vt_ref[...], weighted, preferred_element_type=F32)

    @pl.when(e == pl.num_programs(1) - 1)
    def _():
        out = x_ref[...] + acc_ref[...].T
        if final_norm:
            out = _rmsnorm(out, fw_ref[...])
        o_ref[...] = out


def _experts(ht, n0, w0, r1, e1, u_bf, vt_bf, x2, final_w, final_norm):
    n = x2.shape[0]
    t = PEER_TOKENS
    eb = PEER_EXPERTS_PER_STEP
    ne = u_bf.shape[0] // eb
    hk_spec = pl.BlockSpec((PEER_HEADS, PEER_NKEYS, t), lambda i, e: (0, 0, i))
    ha_spec = pl.BlockSpec((PEER_HEADS, eb // PEER_NKEYS, t), lambda i, e: (0, e, i))
    return pl.pallas_call(
        functools.partial(_expert_kernel, final_norm=final_norm),
        grid=(n // t, ne),
        in_specs=[
            pl.BlockSpec((D_MODEL, t), lambda i, e: (0, i)),
            ha_spec, ha_spec, hk_spec, hk_spec,
            pl.BlockSpec((eb, D_MODEL), lambda i, e: (e, 0)),
            pl.BlockSpec((D_MODEL, eb), lambda i, e: (0, e)),
            pl.BlockSpec((t, D_MODEL), lambda i, e: (i, 0)),
            pl.BlockSpec((1, D_MODEL), lambda i, e: (0, 0)),
        ],
        out_specs=pl.BlockSpec((t, D_MODEL), lambda i, e: (i, 0)),
        out_shape=jax.ShapeDtypeStruct((n, D_MODEL), F32),
        scratch_shapes=[pltpu.VMEM((D_MODEL, t), F32)],
        compiler_params=_params("parallel", "arbitrary"),
        name="peer_experts",
    )(ht, n0, w0, r1, e1, u_bf, vt_bf, x2, final_w)


def _pad_lanes(v, width):
    return jnp.pad(v, ((0, 0), (0, width - v.shape[1])))


def kernel(x, norm1_w, w_in, ssm_conv_w, ssm_conv_b, ssm_dt_bias, ssm_A_log, ssm_D, ssm_norm_w,
           sc_conv_w, w_out, norm2_w, peer_w_q, peer_sub_keys, peer_u, peer_v, final_norm_w):
    batch, seq, d = x.shape
    depth = w_in.shape[0]
    n = batch * seq
    x2 = x.reshape(n, d)

    cuts = [0]
    for width in (SB_WIDTH, SB_WIDTH, SB_WIDTH, SSM_WIDTH, SSM_CONV_DIM, SSM_HEADS,
                  SC_WIDTH, SC_WIDTH, SC_WIDTH):
        cuts.append(cuts[-1] + width)
    ident = jnp.arange(SB_BLOCK)
    upper = (ident[:, None] > ident[None, :]).astype(BF16)
    identm = jnp.arange(MIX_BLOCK)
    tri = (identm[None, :] <= identm[:, None]).astype(BF16)

    for layer in range(depth):
        w = w_in[layer]
        w_cat = jnp.concatenate([
            w[:, cuts[3]:cuts[5]],
            _pad_lanes(w[:, cuts[5]:cuts[6]], DT_PAD),
            w[:, cuts[6]:cuts[9]],
            w[:, cuts[0]:cuts[3]],
        ], axis=1).astype(BF16)
        rest, qkv = _in_proj(x2, norm1_w[layer][None, :], w_cat)

        ysb = _sb_attention(qkv, batch, upper)

        x2 = _mixer(
            rest, ysb, x2, batch,
            ssm_conv_w[layer], ssm_conv_b[layer][None, :],
            _pad_lanes(ssm_dt_bias[layer][None, :], DT_PAD),
            _pad_lanes(ssm_A_log[layer][None, :], DT_PAD),
            jnp.repeat(ssm_D[layer], SSM_HEAD_DIM)[None, :],
            ssm_norm_w[layer][None, :], sc_conv_w[layer],
            w_out[layer].astype(BF16), tri)

        ht, n0, w0, r1, e1 = _route(
            x2, norm2_w[layer][None, :], peer_w_q[layer].T, peer_sub_keys[layer])
        x2 = _experts(ht, n0, w0, r1, e1,
                      (peer_u[layer] * INV_SQRT2).astype(BF16), peer_v[layer].T.astype(BF16),
                      x2, final_norm_w[None, :], layer == depth - 1)
    return x2.reshape(batch, seq, d)
```

```python
import functools
import math

import jax
import jax.numpy as jnp
from jax import lax
from jax.experimental import pallas as pl
from jax.experimental.pallas import tpu as pltpu

F32 = jnp.float32
BF16 = jnp.bfloat16

EPS = 1e-6
D_MODEL = 1024
SB_HEADS = 4
SB_HEAD_DIM = 64
SB_WIDTH = SB_HEADS * SB_HEAD_DIM
SSM_HEADS = 8
SSM_HEAD_DIM = 64
SSM_WIDTH = SSM_HEADS * SSM_HEAD_DIM
SSM_GROUPS = 2
SSM_STATE = 64
SSM_CONV = 4
SSM_CONV_DIM = SSM_WIDTH + 2 * SSM_GROUPS * SSM_STATE
SC_WIDTH = 256
SC_CONV = 3
PEER_HEADS = 4
PEER_KEY_DIM = 256
PEER_NKEYS = 128
PEER_TOPK = 16

LANES = 128
SUBLANES = 8
DT_PAD = LANES
REST_WIDTH = SSM_WIDTH + SSM_CONV_DIM + DT_PAD + 3 * SC_WIDTH
QKV_WIDTH = 3 * SB_WIDTH
VMEM_LIMIT = 56 * 1024 * 1024

IN_PROJ_ROWS = 256
SB_BLOCK = 256
MIX_BLOCK = 256
ROUTE_BLOCK = 256
PEER_TOKENS = 512
PEER_EXPERTS_PER_STEP = 2048
SB_SKIP_LOG = -120.0
INV_SQRT2 = 1.0 / math.sqrt(2.0)
NEG_INF = float("-inf")


def _rmsnorm(x, w):
    return x * lax.rsqrt(jnp.mean(x * x, axis=-1, keepdims=True) + EPS) * w


def _softplus(x):
    return jnp.maximum(x, 0.0) + jnp.log(1.0 + jnp.exp(-jnp.abs(x)))


def _silu(x):
    return x * (1.0 / (1.0 + jnp.exp(-x)))


def _params(*semantics):
    return pltpu.CompilerParams(dimension_semantics=semantics, vmem_limit_bytes=VMEM_LIMIT)


def _in_proj_kernel(x_ref, nw_ref, w_ref, rest_ref, qkv_ref):
    h = _rmsnorm(x_ref[...], nw_ref[...]).astype(BF16)
    p = jnp.dot(h, w_ref[...], preferred_element_type=F32)
    rest_ref[...] = p[:, :REST_WIDTH]
    qkv_ref[...] = p[:, REST_WIDTH:].astype(BF16)


def _in_proj(x2, norm_w, w_cat):
    n = x2.shape[0]
    tm = IN_PROJ_ROWS
    width = w_cat.shape[1]
    return pl.pallas_call(
        _in_proj_kernel,
        grid=(n // tm,),
        in_specs=[
            pl.BlockSpec((tm, D_MODEL), lambda i: (i, 0)),
            pl.BlockSpec((1, D_MODEL), lambda i: (0, 0)),
            pl.BlockSpec((D_MODEL, width), lambda i: (0, 0)),
        ],
        out_specs=[
            pl.BlockSpec((tm, REST_WIDTH), lambda i: (i, 0)),
            pl.BlockSpec((tm, QKV_WIDTH), lambda i: (i, 0)),
        ],
        out_shape=[
            jax.ShapeDtypeStruct((n, REST_WIDTH), F32),
            jax.ShapeDtypeStruct((n, QKV_WIDTH), BF16),
        ],
        compiler_params=_params("parallel"),
        name="in_proj",
    )(x2, norm_w, w_cat)


def _sb_attn_kernel(q_ref, k_ref, v_ref, upper_ref, o_ref):
    t = q_ref.shape[0]
    i = pl.program_id(1)
    upper = upper_ref[...]
    row = lax.broadcasted_iota(jnp.int32, (t, t), 0)
    col = lax.broadcasted_iota(jnp.int32, (t, t), 1)
    lane_head = lax.broadcasted_iota(jnp.int32, (1, SB_WIDTH), 1) // SB_HEAD_DIM
    head_mask = [jnp.where(lane_head == h, 1.0, 0.0).astype(BF16) for h in range(SB_HEADS)]
    qs = q_ref[...] * (SB_HEAD_DIM ** -0.5)
    q_heads = [qs * head_mask[h] for h in range(SB_HEADS)]

    def block(j, accs, out, diagonal):
        start = pl.multiple_of(j * t, t)
        kb = k_ref[pl.ds(start, t), :]
        vb = v_ref[pl.ds(start, t), :]
        new_accs = []
        for h in range(SB_HEADS):
            z = lax.dot_general(q_heads[h], kb, (((1,), (1,)), ((), ())), preferred_element_type=F32)
            log1m = jnp.minimum(-z, 0.0) - jnp.log(1.0 + jnp.exp(-jnp.abs(z)))
            if diagonal:
                mask = col < row
                log1m_m = jnp.where(mask, log1m, 0.0)
            else:
                log1m_m = log1m
            rev = jnp.dot(log1m_m.astype(BF16), upper, preferred_element_type=F32)
            w = jnp.exp(log1m + z + rev + accs[h])
            if diagonal:
                w = jnp.where(mask, w, 0.0)
            out = out + jnp.dot(w.astype(BF16), vb * head_mask[h], preferred_element_type=F32)
            new_accs.append(accs[h] + rev[:, 0:1] + log1m_m[:, 0:1])
        return tuple(new_accs), out

    accs0 = tuple(jnp.zeros((t, 1), F32) for _ in range(SB_HEADS))
    out0 = jnp.zeros((t, SB_WIDTH), F32)
    accs0, out0 = block(i, accs0, out0, True)

    def cond(carry):
        j, accs, _ = carry
        worst = functools.reduce(jnp.maximum, accs)
        return jnp.logical_and(j >= 0, jnp.max(worst) > SB_SKIP_LOG)

    def body(carry):
        j, accs, out = carry
        accs, out = block(j, accs, out, False)
        return j - 1, accs, out

    _, _, out = lax.while_loop(cond, body, (i - 1, accs0, out0))
    o_ref[...] = out


def _sb_attention(qkv, batch, upper):
    n = qkv.shape[0]
    s = n // batch
    t = SB_BLOCK
    nq = s // t
    return pl.pallas_call(
        _sb_attn_kernel,
        grid=(batch, nq),
        in_specs=[
            pl.BlockSpec((t, SB_WIDTH), lambda b, i: (b * nq + i, 0)),
            pl.BlockSpec((s, SB_WIDTH), lambda b, i: (b, 1)),
            pl.BlockSpec((s, SB_WIDTH), lambda b, i: (b, 2)),
            pl.BlockSpec((t, t), lambda b, i: (0, 0)),
        ],
        out_specs=pl.BlockSpec((t, SB_WIDTH), lambda b, i: (b * nq + i, 0)),
        out_shape=jax.ShapeDtypeStruct((n, SB_WIDTH), F32),
        compiler_params=_params("parallel", "arbitrary"),
        name="sb_attention",
    )(qkv, qkv, qkv, upper)


def _mixer_kernel(rest_ref, ysb_ref, x_ref, convw_ref, convb_ref, dtb_ref, alog_ref, dexp_ref,
                  normw_ref, scw_ref, wout_ref, tri_ref, o_ref,
                  xbc_ext, sc_ext, state_ref):
    t = rest_ref.shape[0]
    sblk = pl.program_id(1)
    halo = SUBLANES

    @pl.when(sblk == 0)
    def _():
        xbc_ext[0:halo, :] = jnp.zeros((halo, SSM_CONV_DIM), F32)
        sc_ext[0:halo, :] = jnp.zeros((halo, SC_WIDTH), F32)
        state_ref[...] = jnp.zeros_like(state_ref)

    o_z = 0
    o_xbc = SSM_WIDTH
    o_dt = o_xbc + SSM_CONV_DIM
    o_sc = o_dt + DT_PAD
    z = rest_ref[:, o_z:o_z + SSM_WIDTH]
    dt_raw = rest_ref[:, o_dt:o_dt + DT_PAD]
    sc_b = rest_ref[:, o_sc:o_sc + SC_WIDTH]
    sc_c = rest_ref[:, o_sc + SC_WIDTH:o_sc + 2 * SC_WIDTH]
    sc_v = rest_ref[:, o_sc + 2 * SC_WIDTH:o_sc + 3 * SC_WIDTH]

    xbc_ext[halo:halo + t, :] = rest_ref[:, o_xbc:o_xbc + SSM_CONV_DIM]
    conv = convb_ref[...]
    for kk in range(SSM_CONV):
        shift = SSM_CONV - 1 - kk
        conv = conv + convw_ref[kk:kk + 1, :] * xbc_ext[halo - shift:halo - shift + t, :]
    tail = xbc_ext[t:t + halo, :]
    xbc_ext[0:halo, :] = tail
    xbc = _silu(conv)
    xs = xbc[:, :SSM_WIDTH]

    sc_ext[halo:halo + t, :] = sc_c * sc_v
    sconv = jnp.zeros((t, SC_WIDTH), F32)
    for kk in range(SC_CONV):
        shift = SC_CONV - 1 - kk
        sconv = sconv + scw_ref[kk:kk + 1, :] * sc_ext[halo - shift:halo - shift + t, :]
    sc_tail = sc_ext[t:t + halo, :]
    sc_ext[0:halo, :] = sc_tail
    y_sc = sc_b * sconv

    dt = _softplus(dt_raw + dtb_ref[...])
    a = dt * (-jnp.exp(alog_ref[...]))
    a_hi = a.astype(BF16)
    a_lo = (a - a_hi.astype(F32)).astype(BF16)
    tri = tri_ref[...]
    a_cs = (jnp.dot(tri, a_hi, preferred_element_type=F32)
            + jnp.dot(tri, a_lo, preferred_element_type=F32))
    a_cs_t = a_cs.T
    a_end = a_cs[t - 1:t, :]
    row = lax.broadcasted_iota(jnp.int32, (t, t), 0)
    col = lax.broadcasted_iota(jnp.int32, (t, t), 1)
    causal = row >= col

    def lanes_per_head(cols, h0, nh):
        return jnp.concatenate(
            [jnp.broadcast_to(cols[:, h:h + 1], (cols.shape[0], SSM_HEAD_DIM)) for h in range(h0, h0 + nh)],
            axis=1)

    hpg = SSM_HEADS // SSM_GROUPS
    gw = hpg * SSM_HEAD_DIM
    lane_head = lax.broadcasted_iota(jnp.int32, (t, gw), 1) // SSM_HEAD_DIM
    y_groups = []
    for g in range(SSM_GROUPS):
        b_g = xbc[:, SSM_WIDTH + g * SSM_STATE:SSM_WIDTH + (g + 1) * SSM_STATE]
        c_g = xbc[:, SSM_WIDTH + SSM_GROUPS * SSM_STATE + g * SSM_STATE:
                  SSM_WIDTH + SSM_GROUPS * SSM_STATE + (g + 1) * SSM_STATE]
        b_bf = b_g.astype(BF16)
        c_bf = c_g.astype(BF16)
        xs_g = xs[:, g * gw:(g + 1) * gw]
        dt_g = lanes_per_head(dt, g * hpg, hpg)
        acs_g = lanes_per_head(a_cs, g * hpg, hpg)
        aend_g = lanes_per_head(a_end, g * hpg, hpg)
        xd_g = xs_g * dt_g
        xd_bf = xd_g.astype(BF16)
        cb = lax.dot_general(c_bf, b_bf, (((1,), (1,)), ((), ())), preferred_element_type=F32)
        y_diag = jnp.zeros((t, gw), F32)
        for hh in range(hpg):
            h = g * hpg + hh
            seg = a_cs[:, h:h + 1] - a_cs_t[h:h + 1, :]
            lmat = jnp.exp(jnp.where(causal, seg, NEG_INF))
            p_h = jnp.dot((cb * lmat).astype(BF16), xd_bf, preferred_element_type=F32)
            y_diag = jnp.where(lane_head == hh, p_h, y_diag)
        state = state_ref[g]
        y_off = jnp.dot(c_bf, state.astype(BF16), preferred_element_type=F32) * jnp.exp(acs_g)
        xdd = (xd_g * jnp.exp(aend_g - acs_g)).astype(BF16)
        contrib = lax.dot_general(b_bf, xdd, (((0,), (0,)), ((), ())), preferred_element_type=F32)
        state_ref[g] = state * jnp.exp(aend_g) + contrib
        y_groups.append(y_diag + y_off + dexp_ref[:, g * gw:(g + 1) * gw] * xs_g)
    y = jnp.concatenate(y_groups, axis=1) * _silu(z)
    y_ssm = _rmsnorm(y, normw_ref[...])

    o0 = SB_WIDTH
    o1 = o0 + SSM_WIDTH
    out = x_ref[...]
    out = out + jnp.dot(ysb_ref[...].astype(BF16), wout_ref[0:o0, :], preferred_element_type=F32)
    out = out + jnp.dot(y_ssm.astype(BF16), wout_ref[o0:o1, :], preferred_element_type=F32)
    out = out + jnp.dot(y_sc.astype(BF16), wout_ref[o1:o1 + SC_WIDTH, :], preferred_element_type=F32)
    o_ref[...] = out


def _mixer(rest, ysb, x2, batch, conv_w, conv_b, dt_bias, a_log, d_exp, norm_w, sc_w, w_out, tri):
    n = x2.shape[0]
    t = MIX_BLOCK
    spb = n // batch // t
    row_map = lambda b, s: (b * spb + s, 0)
    const = lambda b, s: (0, 0)
    return pl.pallas_call(
        _mixer_kernel,
        grid=(batch, spb),
        in_specs=[
            pl.BlockSpec((t, REST_WIDTH), row_map),
            pl.BlockSpec((t, SB_WIDTH), row_map),
            pl.BlockSpec((t, D_MODEL), row_map),
            pl.BlockSpec((SSM_CONV, SSM_CONV_DIM), const),
            pl.BlockSpec((1, SSM_CONV_DIM), const),
            pl.BlockSpec((1, DT_PAD), const),
            pl.BlockSpec((1, DT_PAD), const),
            pl.BlockSpec((1, SSM_WIDTH), const),
            pl.BlockSpec((1, SSM_WIDTH), const),
            pl.BlockSpec((SC_CONV, SC_WIDTH), const),
            pl.BlockSpec((D_MODEL, D_MODEL), const),
            pl.BlockSpec((t, t), const),
        ],
        out_specs=pl.BlockSpec((t, D_MODEL), row_map),
        out_shape=jax.ShapeDtypeStruct((n, D_MODEL), F32),
        scratch_shapes=[
            pltpu.VMEM((t + SUBLANES, SSM_CONV_DIM), F32),
            pltpu.VMEM((t + SUBLANES, SC_WIDTH), F32),
            pltpu.VMEM((SSM_GROUPS, SSM_STATE, (SSM_HEADS // SSM_GROUPS) * SSM_HEAD_DIM), F32),
        ],
        compiler_params=_params("arbitrary", "arbitrary"),
        name="mixer",
    )(rest, ysb, x2, conv_w, conv_b, dt_bias, a_log, d_exp, norm_w, sc_w, w_out, tri)


def _sort16_pairs():
    def merge(lo, hi, r):
        step = r * 2
        if step < hi - lo:
            yield from merge(lo, hi, step)
            yield from merge(lo + r, hi, step)
            for i in range(lo + r, hi - r, step):
                yield (i, i + r)
        else:
            yield (lo, lo + r)

    def sort(lo, hi):
        if hi - lo >= 1:
            mid = lo + (hi - lo) // 2
            yield from sort(lo, mid)
            yield from sort(mid + 1, hi)
            yield from merge(lo, hi, 1)

    return tuple(sort(0, PEER_TOPK - 1))


_SORT16 = _sort16_pairs()


def _exchange(v, i, j):
    v[i], v[j] = jnp.maximum(v[i], v[j]), jnp.minimum(v[i], v[j])


def _merge_sublane_lists(v):
    for shift in (4, 2, 1):
        r = [pltpu.roll(x, shift, axis=0) for x in v]
        v = [jnp.maximum(v[i], r[PEER_TOPK - 1 - i]) for i in range(PEER_TOPK)]
        d = PEER_TOPK // 2
        while d >= 1:
            for i in range(PEER_TOPK):
                if not i & d:
                    _exchange(v, i, i + d)
            d //= 2
    return v


def _top16_keys(scores):
    v = [scores[k * SUBLANES:(k + 1) * SUBLANES, :] for k in range(PEER_NKEYS // SUBLANES)]
    for i, j in _SORT16:
        _exchange(v, i, j)
    return _merge_sublane_lists(v)


def _top16_pairs(top0, top1):
    t = top0[0].shape[1]
    sub = lax.broadcasted_iota(jnp.int32, (SUBLANES, t), 0)
    lengths = (16, 8, 5, 4, 12, 4, 1, 0)
    length = jnp.zeros((SUBLANES, t), jnp.int32)
    for s_, n_ in enumerate(lengths):
        length = jnp.where(sub == s_, n_, length)
    row_part = jnp.where(sub == 0, top0[0], jnp.where(sub == 1, top0[1],
                                                     jnp.where(sub == 2, top0[2], top0[3])))
    col_part = jnp.where(sub == 4, top1[0], jnp.where(sub == 5, top1[1], top1[2]))
    lists = []
    for r in range(PEER_TOPK):
        a = jnp.where(sub < 4, row_part, top0[min(4 + r, PEER_TOPK - 1)])
        b = jnp.where(sub < 4, top1[r], col_part)
        lists.append(jnp.where(r < length, a + b, NEG_INF))
    return _merge_sublane_lists(lists)


def _split_bf16(x):
    hi = x.astype(BF16)
    return hi, (x - hi.astype(F32)).astype(BF16)


def _dot3(a_hi, a_lo, b_hi, b_lo):
    return (jnp.dot(a_hi, b_hi, preferred_element_type=F32)
            + jnp.dot(a_hi, b_lo, preferred_element_type=F32)
            + jnp.dot(a_lo, b_hi, preferred_element_type=F32))


def _fold_keys_kernel(keys_ref, wqt_ref, hi_ref, lo_ref):
    k_hi, k_lo = _split_bf16(keys_ref[...])
    w_hi, w_lo = _split_bf16(wqt_ref[...])
    hi_ref[...], lo_ref[...] = _split_bf16(_dot3(k_hi, k_lo, w_hi, w_lo))


def _fold_keys(wq_t, keys):
    half = PEER_KEY_DIM // 2
    nblk = wq_t.shape[0] // half
    out = jax.ShapeDtypeStruct((nblk * PEER_NKEYS, D_MODEL), BF16)
    out_spec = pl.BlockSpec((PEER_NKEYS, D_MODEL), lambda i: (i, 0))
    return pl.pallas_call(
        _fold_keys_kernel,
        grid=(nblk,),
        in_specs=[
            pl.BlockSpec((None, PEER_NKEYS, half), lambda i: (i % 2, 0, 0)),
            pl.BlockSpec((half, D_MODEL), lambda i: (i, 0)),
        ],
        out_specs=[out_spec, out_spec],
        out_shape=[out, out],
        compiler_params=_params("parallel"),
        name="peer_fold_keys",
    )(keys, wq_t)


def _route_kernel(x_ref, nw_ref, wk_hi_ref, wk_lo_ref, ht_ref, n0_ref, w0_ref, r1_ref, e1_ref):
    h = _rmsnorm(x_ref[...], nw_ref[...])
    ht_hi, ht_lo = _split_bf16(h.T)
    ht_ref[...] = ht_hi
    scores = _dot3(wk_hi_ref[...], wk_lo_ref[...], ht_hi, ht_lo)
    for hh in range(PEER_HEADS):
        s0 = scores[(2 * hh) * PEER_NKEYS:(2 * hh + 1) * PEER_NKEYS, :]
        s1 = scores[(2 * hh + 1) * PEER_NKEYS:(2 * hh + 2) * PEER_NKEYS, :]
        for l0 in range(0, scores.shape[1], LANES):
            ln = slice(l0, l0 + LANES)
            _route_lane_tile(s0[:, ln], s1[:, ln], hh, ln, n0_ref, w0_ref, r1_ref, e1_ref)


def _count_leading(t, pred):
    c8 = pred(t[7])
    c4 = pred(jnp.where(c8, t[11], t[3]))
    c2 = pred(jnp.where(c8, jnp.where(c4, t[13], t[9]), jnp.where(c4, t[5], t[1])))
    c1 = pred(jnp.where(c8,
                        jnp.where(c4, jnp.where(c2, t[14], t[12]), jnp.where(c2, t[10], t[8])),
                        jnp.where(c4, jnp.where(c2, t[6], t[4]), jnp.where(c2, t[2], t[0]))))
    count = (jnp.where(c8, 8.0, 0.0) + jnp.where(c4, 4.0, 0.0)
             + jnp.where(c2, 2.0, 0.0) + jnp.where(c1, 1.0, 0.0))
    return count + jnp.where(pred(t[15]), 1.0, 0.0)


def _route_lane_tile(s0, s1, hh, ln, n0_ref, w0_ref, r1_ref, e1_ref):
    top0 = _top16_keys(s0)
    top1 = _top16_keys(s1)
    best = _top16_pairs(top0, top1)
    z = jnp.exp(best[0] - best[0])
    for r in range(1, PEER_TOPK):
        z = z + jnp.exp(best[r] - best[0])
    tau = best[PEER_TOPK - 1]
    scale0 = INV_SQRT2 / z
    r1_rows, e1_rows = [], []
    for k in range(PEER_NKEYS // SUBLANES):
        rows = slice(k * SUBLANES, (k + 1) * SUBLANES)
        s0k = s0[rows, :]
        s1k = s1[rows, :]
        n0_ref[hh, rows, ln] = _count_leading(top1, lambda m: s0k + m >= tau)
        w0_ref[hh, rows, ln] = jnp.exp(s0k - top0[0]) * scale0
        r1_rows.append(_count_leading(top1, lambda m: m > s1k))
        e1_rows.append(jnp.exp(s1k - top1[0]))
    r1_ref[hh, :, ln] = jnp.concatenate(r1_rows, axis=0).astype(BF16)
    e1_ref[hh, :, ln] = jnp.concatenate(e1_rows, axis=0).astype(BF16)


def _route(x2, norm_w, wq_t, keys):
    n = x2.shape[0]
    t = ROUTE_BLOCK
    hk = (PEER_HEADS, PEER_NKEYS, n)
    hk_spec = pl.BlockSpec((PEER_HEADS, PEER_NKEYS, t), lambda i: (0, 0, i))
    wk_hi, wk_lo = _fold_keys(wq_t, keys)
    wk_spec = pl.BlockSpec(wk_hi.shape, lambda i: (0, 0))
    return pl.pallas_call(
        _route_kernel,
        grid=(n // t,),
        in_specs=[
            pl.BlockSpec((t, D_MODEL), lambda i: (i, 0)),
            pl.BlockSpec((1, D_MODEL), lambda i: (0, 0)),
            wk_spec, wk_spec,
        ],
        out_specs=[
            pl.BlockSpec((D_MODEL, t), lambda i: (0, i)),
            hk_spec, hk_spec, hk_spec, hk_spec,
        ],
        out_shape=[
            jax.ShapeDtypeStruct((D_MODEL, n), BF16),
            jax.ShapeDtypeStruct(hk, F32), jax.ShapeDtypeStruct(hk, F32),
            jax.ShapeDtypeStruct(hk, BF16), jax.ShapeDtypeStruct(hk, BF16),
        ],
        compiler_params=_params("parallel"),
        name="peer_route",
    )(x2, norm_w, wk_hi, wk_lo)


def _bf16_rows(row, n):
    packed = 2 * SUBLANES
    tile = jnp.broadcast_to(row, (packed, row.shape[1])).astype(BF16)
    return jnp.concatenate([tile] * (n // packed), axis=0)


def _expert_kernel(ht_ref, n0_ref, w0_ref, r1_ref, e1_ref, u_ref, vt_ref, x_ref, fw_ref,
                   o_ref, acc_ref, *, final_norm):
    e = pl.program_id(1)
    eb = u_ref.shape[0]
    t = ht_ref.shape[1]

    @pl.when(e == 0)
    def _():
        acc_ref[...] = jnp.zeros_like(acc_ref)

    def gated(y, sub):
        gate = jnp.zeros((PEER_NKEYS, t), BF16)
        for hh in range(PEER_HEADS):
            n0 = _bf16_rows(n0_ref[hh, sub:sub + 1, :], PEER_NKEYS)
            w0 = _bf16_rows(w0_ref[hh, sub:sub + 1, :], PEER_NKEYS)
            gate = gate + jnp.where(r1_ref[hh] < n0, e1_ref[hh], 0.0) * w0
        return gate * (y * (1.0 + lax.erf(y))).astype(BF16)

    half = eb // 2
    nsub = half // PEER_NKEYS
    ht = ht_ref[...]
    y_a = jnp.dot(u_ref[0:half, :], ht, preferred_element_type=F32)
    rows_a = [gated(y_a[s * PEER_NKEYS:(s + 1) * PEER_NKEYS, :], s) for s in range(nsub)]
    bits = pltpu.bitcast(rows_a[0][0:2 * SUBLANES, 0:LANES], jnp.uint32)
    sixteen = jnp.uint32(16)
    zero = pltpu.bitcast(lax.shift_right_logical(lax.shift_right_logical(bits, sixteen), sixteen), BF16)
    ht_b = ht + jnp.tile(zero, (ht.shape[0] // zero.shape[0], t // LANES))
    y_b = jnp.dot(u_ref[half:eb, :], ht_b, preferred_element_type=F32)
    rows_b = [gated(y_b[s * PEER_NKEYS:(s + 1) * PEER_NKEYS, :], nsub + s) for s in range(nsub)]
    weighted = jnp.concatenate(rows_a + rows_b, axis=0)
    acc_ref[...] += jnp.dot(vt_ref[...], weighted, preferred_element_type=F32)

    @pl.when(e == pl.num_programs(1) - 1)
    def _():
        out = x_ref[...] + acc_ref[...].T
        if final_norm:
            out = _rmsnorm(out, fw_ref[...])
        o_ref[...] = out


def _experts(ht, n0, w0, r1, e1, u_bf, vt_bf, x2, final_w, final_norm):
    n = x2.shape[0]
    t = PEER_TOKENS
    eb = PEER_EXPERTS_PER_STEP
    ne = u_bf.shape[0] // eb
    hk_spec = pl.BlockSpec((PEER_HEADS, PEER_NKEYS, t), lambda i, e: (0, 0, i))
    ha_spec = pl.BlockSpec((PEER_HEADS, eb // PEER_NKEYS, t), lambda i, e: (0, e, i))
    return pl.pallas_call(
        functools.partial(_expert_kernel, final_norm=final_norm),
        grid=(n // t, ne),
        in_specs=[
            pl.BlockSpec((D_MODEL, t), lambda i, e: (0, i)),
            ha_spec, ha_spec, hk_spec, hk_spec,
            pl.BlockSpec((eb, D_MODEL), lambda i, e: (e, 0)),
            pl.BlockSpec((D_MODEL, eb), lambda i, e: (0, e)),
            pl.BlockSpec((t, D_MODEL), lambda i, e: (i, 0)),
            pl.BlockSpec((1, D_MODEL), lambda i, e: (0, 0)),
        ],
        out_specs=pl.BlockSpec((t, D_MODEL), lambda i, e: (i, 0)),
        out_shape=jax.ShapeDtypeStruct((n, D_MODEL), F32),
        scratch_shapes=[pltpu.VMEM((D_MODEL, t), F32)],
        compiler_params=_params("parallel", "arbitrary"),
        name="peer_experts",
    )(ht, n0, w0, r1, e1, u_bf, vt_bf, x2, final_w)


def _pad_lanes(v, width):
    return jnp.pad(v, ((0, 0), (0, width - v.shape[1])))


def kernel(x, norm1_w, w_in, ssm_conv_w, ssm_conv_b, ssm_dt_bias, ssm_A_log, ssm_D, ssm_norm_w,
           sc_conv_w, w_out, norm2_w, peer_w_q, peer_sub_keys, peer_u, peer_v, final_norm_w):
    batch, seq, d = x.shape
    depth = w_in.shape[0]
    n = batch * seq
    x2 = x.reshape(n, d)

    cuts = [0]
    for width in (SB_WIDTH, SB_WIDTH, SB_WIDTH, SSM_WIDTH, SSM_CONV_DIM, SSM_HEADS,
                  SC_WIDTH, SC_WIDTH, SC_WIDTH):
        cuts.append(cuts[-1] + width)
    ident = jnp.arange(SB_BLOCK)
    upper = (ident[:, None] > ident[None, :]).astype(BF16)
    identm = jnp.arange(MIX_BLOCK)
    tri = (identm[None, :] <= identm[:, None]).astype(BF16)

    for layer in range(depth):
        w = w_in[layer]
        w_cat = jnp.concatenate([
            w[:, cuts[3]:cuts[5]],
            _pad_lanes(w[:, cuts[5]:cuts[6]], DT_PAD),
            w[:, cuts[6]:cuts[9]],
            w[:, cuts[0]:cuts[3]],
        ], axis=1).astype(BF16)
        rest, qkv = _in_proj(x2, norm1_w[layer][None, :], w_cat)

        ysb = _sb_attention(qkv, batch, upper)

        x2 = _mixer(
            rest, ysb, x2, batch,
            ssm_conv_w[layer], ssm_conv_b[layer][None, :],
            _pad_lanes(ssm_dt_bias[layer][None, :], DT_PAD),
            _pad_lanes(ssm_A_log[layer][None, :], DT_PAD),
            jnp.repeat(ssm_D[layer], SSM_HEAD_DIM)[None, :],
            ssm_norm_w[layer][None, :], sc_conv_w[layer],
            w_out[layer].astype(BF16), tri)

        ht, n0, w0, r1, e1 = _route(
            x2, norm2_w[layer][None, :], peer_w_q[layer].T, peer_sub_keys[layer])
        x2 = _experts(ht, n0, w0, r1, e1,
                      (peer_u[layer] * INV_SQRT2).astype(BF16), peer_v[layer].T.astype(BF16),
                      x2, final_norm_w[None, :], layer == depth - 1)
    return x2.reshape(batch, seq, d)
```

```python
import functools
import math

import jax
import jax.numpy as jnp
from jax import lax
from jax.experimental import pallas as pl
from jax.experimental.pallas import tpu as pltpu

F32 = jnp.float32
BF16 = jnp.bfloat16

EPS = 1e-6
D_MODEL = 1024
SB_HEADS = 4
SB_HEAD_DIM = 64
SB_WIDTH = SB_HEADS * SB_HEAD_DIM
SSM_HEADS = 8
SSM_HEAD_DIM = 64
SSM_WIDTH = SSM_HEADS * SSM_HEAD_DIM
SSM_GROUPS = 2
SSM_STATE = 64
SSM_CONV = 4
SSM_CONV_DIM = SSM_WIDTH + 2 * SSM_GROUPS * SSM_STATE
SC_WIDTH = 256
SC_CONV = 3
PEER_HEADS = 4
PEER_KEY_DIM = 256
PEER_NKEYS = 128
PEER_TOPK = 16

LANES = 128
SUBLANES = 8
DT_PAD = LANES
REST_WIDTH = SSM_WIDTH + SSM_CONV_DIM + DT_PAD + 3 * SC_WIDTH
QKV_WIDTH = 3 * SB_WIDTH
VMEM_LIMIT = 56 * 1024 * 1024

IN_PROJ_ROWS = 512
SB_BLOCK = 256
MIX_BLOCK = 256
ROUTE_BLOCK = 256
PEER_TOKENS = 512
PEER_EXPERTS_PER_STEP = 2048
SB_SKIP_LOG = -120.0
INV_SQRT2 = 1.0 / math.sqrt(2.0)
NEG_INF = float("-inf")


def _rmsnorm(x, w):
    return x * lax.rsqrt(jnp.mean(x * x, axis=-1, keepdims=True) + EPS) * w


def _softplus(x):
    return jnp.maximum(x, 0.0) + jnp.log(1.0 + jnp.exp(-jnp.abs(x)))


def _silu(x):
    return x * (1.0 / (1.0 + jnp.exp(-x)))


def _params(*semantics):
    return pltpu.CompilerParams(dimension_semantics=semantics, vmem_limit_bytes=VMEM_LIMIT)


def _in_proj_kernel(x_ref, nw_ref, w_ref, rest_ref, qkv_ref):
    h = _rmsnorm(x_ref[...], nw_ref[...]).astype(BF16)
    p = jnp.dot(h, w_ref[...], preferred_element_type=F32)
    rest_ref[...] = p[:, :REST_WIDTH]
    qkv_ref[...] = p[:, REST_WIDTH:].astype(BF16)


def _in_proj(x2, norm_w, w_cat):
    n = x2.shape[0]
    tm = IN_PROJ_ROWS
    width = w_cat.shape[1]
    return pl.pallas_call(
        _in_proj_kernel,
        grid=(n // tm,),
        in_specs=[
            pl.BlockSpec((tm, D_MODEL), lambda i: (i, 0)),
            pl.BlockSpec((1, D_MODEL), lambda i: (0, 0)),
            pl.BlockSpec((D_MODEL, width), lambda i: (0, 0)),
        ],
        out_specs=[
            pl.BlockSpec((tm, REST_WIDTH), lambda i: (i, 0)),
            pl.BlockSpec((tm, QKV_WIDTH), lambda i: (i, 0)),
        ],
        out_shape=[
            jax.ShapeDtypeStruct((n, REST_WIDTH), F32),
            jax.ShapeDtypeStruct((n, QKV_WIDTH), BF16),
        ],
        compiler_params=_params("parallel"),
        name="in_proj",
    )(x2, norm_w, w_cat)


def _sb_attn_kernel(q_ref, k_ref, v_ref, upper_ref, o_ref):
    t = q_ref.shape[0]
    i = pl.program_id(1)
    upper = upper_ref[...]
    row = lax.broadcasted_iota(jnp.int32, (t, t), 0)
    col = lax.broadcasted_iota(jnp.int32, (t, t), 1)
    lane_head = lax.broadcasted_iota(jnp.int32, (1, SB_WIDTH), 1) // SB_HEAD_DIM
    head_mask = [jnp.where(lane_head == h, 1.0, 0.0).astype(BF16) for h in range(SB_HEADS)]
    qs = q_ref[...] * (SB_HEAD_DIM ** -0.5)
    q_heads = [qs * head_mask[h] for h in range(SB_HEADS)]

    def block(j, accs, out, diagonal):
        start = pl.multiple_of(j * t, t)
        kb = k_ref[pl.ds(start, t), :]
        vb = v_ref[pl.ds(start, t), :]
        new_accs = []
        for h in range(SB_HEADS):
            z = lax.dot_general(q_heads[h], kb, (((1,), (1,)), ((), ())), preferred_element_type=F32)
            log1m = jnp.minimum(-z, 0.0) - jnp.log(1.0 + jnp.exp(-jnp.abs(z)))
            if diagonal:
                mask = col < row
                log1m_m = jnp.where(mask, log1m, 0.0)
            else:
                log1m_m = log1m
            rev = jnp.dot(log1m_m.astype(BF16), upper, preferred_element_type=F32)
            w = jnp.exp(log1m + z + rev + accs[h])
            if diagonal:
                w = jnp.where(mask, w, 0.0)
            out = out + jnp.dot(w.astype(BF16), vb * head_mask[h], preferred_element_type=F32)
            new_accs.append(accs[h] + rev[:, 0:1] + log1m_m[:, 0:1])
        return tuple(new_accs), out

    accs0 = tuple(jnp.zeros((t, 1), F32) for _ in range(SB_HEADS))
    out0 = jnp.zeros((t, SB_WIDTH), F32)
    accs0, out0 = block(i, accs0, out0, True)
    accs1, out1 = block(jnp.maximum(i - 1, 0), accs0, jnp.zeros((t, SB_WIDTH), F32), False)
    has_prev = i > 0
    accs0 = tuple(jnp.where(has_prev, a1, a0) for a0, a1 in zip(accs0, accs1))
    out0 = out0 + jnp.where(has_prev, out1, 0.0)

    def cond(carry):
        j, accs, _ = carry
        worst = functools.reduce(jnp.maximum, accs)
        return jnp.logical_and(j >= 0, jnp.max(worst) > SB_SKIP_LOG)

    def body(carry):
        j, accs, out = carry
        accs, out = block(j, accs, out, False)
        return j - 1, accs, out

    _, _, out = lax.while_loop(cond, body, (i - 2, accs0, out0))
    o_ref[...] = out


def _sb_attention(qkv, batch, upper):
    n = qkv.shape[0]
    s = n // batch
    t = SB_BLOCK
    nq = s // t
    return pl.pallas_call(
        _sb_attn_kernel,
        grid=(batch, nq),
        in_specs=[
            pl.BlockSpec((t, SB_WIDTH), lambda b, i: (b * nq + i, 0)),
            pl.BlockSpec((s, SB_WIDTH), lambda b, i: (b, 1)),
            pl.BlockSpec((s, SB_WIDTH), lambda b, i: (b, 2)),
            pl.BlockSpec((t, t), lambda b, i: (0, 0)),
        ],
        out_specs=pl.BlockSpec((t, SB_WIDTH), lambda b, i: (b * nq + i, 0)),
        out_shape=jax.ShapeDtypeStruct((n, SB_WIDTH), F32),
        compiler_params=_params("parallel", "arbitrary"),
        name="sb_attention",
    )(qkv, qkv, qkv, upper)


def _mixer_kernel(rest_ref, ysb_ref, x_ref, convw_ref, convb_ref, dtb_ref, alog_ref, dexp_ref,
                  normw_ref, scw_ref, wout_ref, tri_ref, o_ref,
                  xbc_ext, sc_ext, state_ref):
    t = rest_ref.shape[0]
    sblk = pl.program_id(1)
    halo = SUBLANES

    @pl.when(sblk == 0)
    def _():
        xbc_ext[0:halo, :] = jnp.zeros((halo, SSM_CONV_DIM), F32)
        sc_ext[0:halo, :] = jnp.zeros((halo, SC_WIDTH), F32)
        state_ref[...] = jnp.zeros_like(state_ref)

    o_z = 0
    o_xbc = SSM_WIDTH
    o_dt = o_xbc + SSM_CONV_DIM
    o_sc = o_dt + DT_PAD
    z = rest_ref[:, o_z:o_z + SSM_WIDTH]
    dt_raw = rest_ref[:, o_dt:o_dt + DT_PAD]
    sc_b = rest_ref[:, o_sc:o_sc + SC_WIDTH]
    sc_c = rest_ref[:, o_sc + SC_WIDTH:o_sc + 2 * SC_WIDTH]
    sc_v = rest_ref[:, o_sc + 2 * SC_WIDTH:o_sc + 3 * SC_WIDTH]

    xbc_ext[halo:halo + t, :] = rest_ref[:, o_xbc:o_xbc + SSM_CONV_DIM]
    conv = convb_ref[...]
    for kk in range(SSM_CONV):
        shift = SSM_CONV - 1 - kk
        conv = conv + convw_ref[kk:kk + 1, :] * xbc_ext[halo - shift:halo - shift + t, :]
    tail = xbc_ext[t:t + halo, :]
    xbc_ext[0:halo, :] = tail
    xbc = _silu(conv)
    xs = xbc[:, :SSM_WIDTH]

    sc_ext[halo:halo + t, :] = sc_c * sc_v
    sconv = jnp.zeros((t, SC_WIDTH), F32)
    for kk in range(SC_CONV):
        shift = SC_CONV - 1 - kk
        sconv = sconv + scw_ref[kk:kk + 1, :] * sc_ext[halo - shift:halo - shift + t, :]
    sc_tail = sc_ext[t:t + halo, :]
    sc_ext[0:halo, :] = sc_tail
    y_sc = sc_b * sconv

    dt = _softplus(dt_raw + dtb_ref[...])
    a = dt * (-jnp.exp(alog_ref[...]))
    a_hi = a.astype(BF16)
    a_lo = (a - a_hi.astype(F32)).astype(BF16)
    tri = tri_ref[...]
    a_cs = (jnp.dot(tri, a_hi, preferred_element_type=F32)
            + jnp.dot(tri, a_lo, preferred_element_type=F32))
    a_cs_t = a_cs.T
    a_end = a_cs[t - 1:t, :]
    row = lax.broadcasted_iota(jnp.int32, (t, t), 0)
    col = lax.broadcasted_iota(jnp.int32, (t, t), 1)
    causal = row >= col

    def lanes_per_head(cols, h0, nh):
        return jnp.concatenate(
            [jnp.broadcast_to(cols[:, h:h + 1], (cols.shape[0], SSM_HEAD_DIM)) for h in range(h0, h0 + nh)],
            axis=1)

    hpg = SSM_HEADS // SSM_GROUPS
    gw = hpg * SSM_HEAD_DIM
    lane_head = lax.broadcasted_iota(jnp.int32, (t, gw), 1) // SSM_HEAD_DIM
    y_groups = []
    for g in range(SSM_GROUPS):
        b_g = xbc[:, SSM_WIDTH + g * SSM_STATE:SSM_WIDTH + (g + 1) * SSM_STATE]
        c_g = xbc[:, SSM_WIDTH + SSM_GROUPS * SSM_STATE + g * SSM_STATE:
                  SSM_WIDTH + SSM_GROUPS * SSM_STATE + (g + 1) * SSM_STATE]
        b_bf = b_g.astype(BF16)
        c_bf = c_g.astype(BF16)
        xs_g = xs[:, g * gw:(g + 1) * gw]
        dt_g = lanes_per_head(dt, g * hpg, hpg)
        acs_g = lanes_per_head(a_cs, g * hpg, hpg)
        aend_g = lanes_per_head(a_end, g * hpg, hpg)
        xd_g = xs_g * dt_g
        xd_bf = xd_g.astype(BF16)
        cb = lax.dot_general(c_bf, b_bf, (((1,), (1,)), ((), ())), preferred_element_type=F32)
        y_diag = jnp.zeros((t, gw), F32)
        for hh in range(hpg):
            h = g * hpg + hh
            seg = a_cs[:, h:h + 1] - a_cs_t[h:h + 1, :]
            lmat = jnp.exp(jnp.where(causal, seg, NEG_INF))
            p_h = jnp.dot((cb * lmat).astype(BF16), xd_bf, preferred_element_type=F32)
            y_diag = jnp.where(lane_head == hh, p_h, y_diag)
        state = state_ref[g]
        y_off = jnp.dot(c_bf, state.astype(BF16), preferred_element_type=F32) * jnp.exp(acs_g)
        xdd = (xd_g * jnp.exp(aend_g - acs_g)).astype(BF16)
        contrib = lax.dot_general(b_bf, xdd, (((0,), (0,)), ((), ())), preferred_element_type=F32)
        state_ref[g] = state * jnp.exp(aend_g) + contrib
        y_groups.append(y_diag + y_off + dexp_ref[:, g * gw:(g + 1) * gw] * xs_g)
    y = jnp.concatenate(y_groups, axis=1) * _silu(z)
    y_ssm = _rmsnorm(y, normw_ref[...])

    o0 = SB_WIDTH
    o1 = o0 + SSM_WIDTH
    out = x_ref[...]
    out = out + jnp.dot(ysb_ref[...].astype(BF16), wout_ref[0:o0, :], preferred_element_type=F32)
    out = out + jnp.dot(y_ssm.astype(BF16), wout_ref[o0:o1, :], preferred_element_type=F32)
    out = out + jnp.dot(y_sc.astype(BF16), wout_ref[o1:o1 + SC_WIDTH, :], preferred_element_type=F32)
    o_ref[...] = out


def _mixer(rest, ysb, x2, batch, conv_w, conv_b, dt_bias, a_log, d_exp, norm_w, sc_w, w_out, tri):
    n = x2.shape[0]
    t = MIX_BLOCK
    spb = n // batch // t
    row_map = lambda b, s: (b * spb + s, 0)
    const = lambda b, s: (0, 0)
    return pl.pallas_call(
        _mixer_kernel,
        grid=(batch, spb),
        in_specs=[
            pl.BlockSpec((t, REST_WIDTH), row_map),
            pl.BlockSpec((t, SB_WIDTH), row_map),
            pl.BlockSpec((t, D_MODEL), row_map),
            pl.BlockSpec((SSM_CONV, SSM_CONV_DIM), const),
            pl.BlockSpec((1, SSM_CONV_DIM), const),
            pl.BlockSpec((1, DT_PAD), const),
            pl.BlockSpec((1, DT_PAD), const),
            pl.BlockSpec((1, SSM_WIDTH), const),
            pl.BlockSpec((1, SSM_WIDTH), const),
            pl.BlockSpec((SC_CONV, SC_WIDTH), const),
            pl.BlockSpec((D_MODEL, D_MODEL), const),
            pl.BlockSpec((t, t), const),
        ],
        out_specs=pl.BlockSpec((t, D_MODEL), row_map),
        out_shape=jax.ShapeDtypeStruct((n, D_MODEL), F32),
        scratch_shapes=[
            pltpu.VMEM((t + SUBLANES, SSM_CONV_DIM), F32),
            pltpu.VMEM((t + SUBLANES, SC_WIDTH), F32),
            pltpu.VMEM((SSM_GROUPS, SSM_STATE, (SSM_HEADS // SSM_GROUPS) * SSM_HEAD_DIM), F32),
        ],
        compiler_params=_params("arbitrary", "arbitrary"),
        name="mixer",
    )(rest, ysb, x2, conv_w, conv_b, dt_bias, a_log, d_exp, norm_w, sc_w, w_out, tri)


def _sort16_pairs():
    def merge(lo, hi, r):
        step = r * 2
        if step < hi - lo:
            yield from merge(lo, hi, step)
            yield from merge(lo + r, hi, step)
            for i in range(lo + r, hi - r, step):
                yield (i, i + r)
        else:
            yield (lo, lo + r)

    def sort(lo, hi):
        if hi - lo >= 1:
            mid = lo + (hi - lo) // 2
            yield from sort(lo, mid)
            yield from sort(mid + 1, hi)
            yield from merge(lo, hi, 1)

    return tuple(sort(0, PEER_TOPK - 1))


_SORT16 = _sort16_pairs()


def _exchange(v, i, j):
    v[i], v[j] = jnp.maximum(v[i], v[j]), jnp.minimum(v[i], v[j])


def _merge_sublane_lists(v):
    for shift in (4, 2, 1):
        r = [pltpu.roll(x, shift, axis=0) for x in v]
        v = [jnp.maximum(v[i], r[PEER_TOPK - 1 - i]) for i in range(PEER_TOPK)]
        d = PEER_TOPK // 2
        while d >= 1:
            for i in range(PEER_TOPK):
                if not i & d:
                    _exchange(v, i, i + d)
            d //= 2
    return v


def _top16_keys(scores):
    v = [scores[k * SUBLANES:(k + 1) * SUBLANES, :] for k in range(PEER_NKEYS // SUBLANES)]
    for i, j in _SORT16:
        _exchange(v, i, j)
    return _merge_sublane_lists(v)


def _top16_pairs(top0, top1):
    t = top0[0].shape[1]
    sub = lax.broadcasted_iota(jnp.int32, (SUBLANES, t), 0)
    lengths = (16, 8, 5, 4, 12, 4, 1, 0)
    length = jnp.zeros((SUBLANES, t), jnp.int32)
    for s_, n_ in enumerate(lengths):
        length = jnp.where(sub == s_, n_, length)
    row_part = jnp.where(sub == 0, top0[0], jnp.where(sub == 1, top0[1],
                                                     jnp.where(sub == 2, top0[2], top0[3])))
    col_part = jnp.where(sub == 4, top1[0], jnp.where(sub == 5, top1[1], top1[2]))
    lists = []
    for r in range(PEER_TOPK):
        a = jnp.where(sub < 4, row_part, top0[min(4 + r, PEER_TOPK - 1)])
        b = jnp.where(sub < 4, top1[r], col_part)
        lists.append(jnp.where(r < length, a + b, NEG_INF))
    return _merge_sublane_lists(lists)


def _split_bf16(x):
    hi = x.astype(BF16)
    return hi, (x - hi.astype(F32)).astype(BF16)


def _dot3(a_hi, a_lo, b_hi, b_lo):
    return (jnp.dot(a_hi, b_hi, preferred_element_type=F32)
            + jnp.dot(a_hi, b_lo, preferred_element_type=F32)
            + jnp.dot(a_lo, b_hi, preferred_element_type=F32))


def _fold_keys_kernel(keys_ref, wqt_ref, hi_ref, lo_ref):
    k_hi, k_lo = _split_bf16(keys_ref[...])
    w_hi, w_lo = _split_bf16(wqt_ref[...])
    hi_ref[...], lo_ref[...] = _split_bf16(_dot3(k_hi, k_lo, w_hi, w_lo))


def _fold_keys(wq_t, keys):
    half = PEER_KEY_DIM // 2
    nblk = wq_t.shape[0] // half
    out = jax.ShapeDtypeStruct((nblk * PEER_NKEYS, D_MODEL), BF16)
    out_spec = pl.BlockSpec((PEER_NKEYS, D_MODEL), lambda i: (i, 0))
    return pl.pallas_call(
        _fold_keys_kernel,
        grid=(nblk,),
        in_specs=[
            pl.BlockSpec((None, PEER_NKEYS, half), lambda i: (i % 2, 0, 0)),
            pl.BlockSpec((half, D_MODEL), lambda i: (i, 0)),
        ],
        out_specs=[out_spec, out_spec],
        out_shape=[out, out],
        compiler_params=_params("parallel"),
        name="peer_fold_keys",
    )(keys, wq_t)


def _route_kernel(x_ref, nw_ref, wk_hi_ref, wk_lo_ref, ht_ref, n0_ref, w0_ref, r1_ref, e1_ref):
    h = _rmsnorm(x_ref[...], nw_ref[...])
    ht_hi, ht_lo = _split_bf16(h.T)
    ht_ref[...] = ht_hi
    scores = _dot3(wk_hi_ref[...], wk_lo_ref[...], ht_hi, ht_lo)
    for hh in range(PEER_HEADS):
        s0 = scores[(2 * hh) * PEER_NKEYS:(2 * hh + 1) * PEER_NKEYS, :]
        s1 = scores[(2 * hh + 1) * PEER_NKEYS:(2 * hh + 2) * PEER_NKEYS, :]
        for l0 in range(0, scores.shape[1], LANES):
            ln = slice(l0, l0 + LANES)
            _route_lane_tile(s0[:, ln], s1[:, ln], hh, ln, n0_ref, w0_ref, r1_ref, e1_ref)


def _count_leading(t, pred):
    c8 = pred(t[7])
    c4 = pred(jnp.where(c8, t[11], t[3]))
    c2 = pred(jnp.where(c8, jnp.where(c4, t[13], t[9]), jnp.where(c4, t[5], t[1])))
    c1 = pred(jnp.where(c8,
                        jnp.where(c4, jnp.where(c2, t[14], t[12]), jnp.where(c2, t[10], t[8])),
                        jnp.where(c4, jnp.where(c2, t[6], t[4]), jnp.where(c2, t[2], t[0]))))
    count = (jnp.where(c8, 8.0, 0.0) + jnp.where(c4, 4.0, 0.0)
             + jnp.where(c2, 2.0, 0.0) + jnp.where(c1, 1.0, 0.0))
    return count + jnp.where(pred(t[15]), 1.0, 0.0)


def _route_lane_tile(s0, s1, hh, ln, n0_ref, w0_ref, r1_ref, e1_ref):
    top0 = _top16_keys(s0)
    top1 = _top16_keys(s1)
    best = _top16_pairs(top0, top1)
    z = jnp.exp(best[0] - best[0])
    for r in range(1, PEER_TOPK):
        z = z + jnp.exp(best[r] - best[0])
    tau = best[PEER_TOPK - 1]
    scale0 = INV_SQRT2 / z
    r1_rows, e1_rows = [], []
    for k in range(PEER_NKEYS // SUBLANES):
        rows = slice(k * SUBLANES, (k + 1) * SUBLANES)
        s0k = s0[rows, :]
        s1k = s1[rows, :]
        n0_ref[hh, rows, ln] = _count_leading(top1, lambda m: s0k + m >= tau)
        w0_ref[hh, rows, ln] = jnp.exp(s0k - top0[0]) * scale0
        r1_rows.append(_count_leading(top1, lambda m: m > s1k))
        e1_rows.append(jnp.exp(s1k - top1[0]))
    r1_ref[hh, :, ln] = jnp.concatenate(r1_rows, axis=0).astype(BF16)
    e1_ref[hh, :, ln] = jnp.concatenate(e1_rows, axis=0).astype(BF16)


def _route(x2, norm_w, wq_t, keys):
    n = x2.shape[0]
    t = ROUTE_BLOCK
    hk = (PEER_HEADS, PEER_NKEYS, n)
    hk_spec = pl.BlockSpec((PEER_HEADS, PEER_NKEYS, t), lambda i: (0, 0, i))
    wk_hi, wk_lo = _fold_keys(wq_t, keys)
    wk_spec = pl.BlockSpec(wk_hi.shape, lambda i: (0, 0))
    return pl.pallas_call(
        _route_kernel,
        grid=(n // t,),
        in_specs=[
            pl.BlockSpec((t, D_MODEL), lambda i: (i, 0)),
            pl.BlockSpec((1, D_MODEL), lambda i: (0, 0)),
            wk_spec, wk_spec,
        ],
        out_specs=[
            pl.BlockSpec((D_MODEL, t), lambda i: (0, i)),
            hk_spec, hk_spec, hk_spec, hk_spec,
        ],
        out_shape=[
            jax.ShapeDtypeStruct((D_MODEL, n), BF16),
            jax.ShapeDtypeStruct(hk, F32), jax.ShapeDtypeStruct(hk, F32),
            jax.ShapeDtypeStruct(hk, BF16), jax.ShapeDtypeStruct(hk, BF16),
        ],
        compiler_params=_params("parallel"),
        name="peer_route",
    )(x2, norm_w, wk_hi, wk_lo)


def _bf16_rows(row, n):
    packed = 2 * SUBLANES
    tile = jnp.broadcast_to(row, (packed, row.shape[1])).astype(BF16)
    return jnp.concatenate([tile] * (n // packed), axis=0)


def _expert_kernel(ht_ref, n0_ref, w0_ref, r1_ref, e1_ref, u_ref, vt_ref, x_ref, fw_ref,
                   o_ref, acc_ref, *, final_norm):
    e = pl.program_id(1)
    eb = u_ref.shape[0]
    t = ht_ref.shape[1]

    @pl.when(e == 0)
    def _():
        acc_ref[...] = jnp.zeros_like(acc_ref)

    def gated(y, sub):
        gate = jnp.zeros((PEER_NKEYS, t), BF16)
        for hh in range(PEER_HEADS):
            n0 = _bf16_rows(n0_ref[hh, sub:sub + 1, :], PEER_NKEYS)
            w0 = _bf16_rows(w0_ref[hh, sub:sub + 1, :], PEER_NKEYS)
            gate = gate + jnp.where(r1_ref[hh] < n0, e1_ref[hh], 0.0) * w0
        return gate * (y * (1.0 + lax.erf(y))).astype(BF16)

    half = eb // 2
    nsub = half // PEER_NKEYS
    ht = ht_ref[...]
    y_a = jnp.dot(u_ref[0:half, :], ht, preferred_element_type=F32)
    rows_a = [gated(y_a[s * PEER_NKEYS:(s + 1) * PEER_NKEYS, :], s) for s in range(nsub)]
    bits = pltpu.bitcast(rows_a[0][0:2 * SUBLANES, 0:LANES], jnp.uint32)
    sixteen = jnp.uint32(16)
    zero = pltpu.bitcast(lax.shift_right_logical(lax.shift_right_logical(bits, sixteen), sixteen), BF16)
    ht_b = ht + jnp.tile(zero, (ht.shape[0] // zero.shape[0], t // LANES))
    y_b = jnp.dot(u_ref[half:eb, :], ht_b, preferred_element_type=F32)
    rows_b = [gated(y_b[s * PEER_NKEYS:(s + 1) * PEER_NKEYS, :], nsub + s) for s in range(nsub)]
    weighted = jnp.concatenate(rows_a + rows_b, axis=0)
    acc_ref[...] += jnp.dot(vt_ref[...], weighted, preferred_element_type=F32)

    @pl.when(e == pl.num_programs(1) - 1)
    def _():
        out = x_ref[...] + acc_ref[...].T
        if final_norm:
            out = _rmsnorm(out, fw_ref[...])
        o_ref[...] = out


def _experts(ht, n0, w0, r1, e1, u_bf, vt_bf, x2, final_w, final_norm):
    n = x2.shape[0]
    t = PEER_TOKENS
    eb = PEER_EXPERTS_PER_STEP
    ne = u_bf.shape[0] // eb
    hk_spec = pl.BlockSpec((PEER_HEADS, PEER_NKEYS, t), lambda i, e: (0, 0, i))
    ha_spec = pl.BlockSpec((PEER_HEADS, eb // PEER_NKEYS, t), lambda i, e: (0, e, i))
    return pl.pallas_call(
        functools.partial(_expert_kernel, final_norm=final_norm),
        grid=(n // t, ne),
        in_specs=[
            pl.BlockSpec((D_MODEL, t), lambda i, e: (0, i)),
            ha_spec, ha_spec, hk_spec, hk_spec,
            pl.BlockSpec((eb, D_MODEL), lambda i, e: (e, 0)),
            pl.BlockSpec((D_MODEL, eb), lambda i, e: (0, e)),
            pl.BlockSpec((t, D_MODEL), lambda i, e: (i, 0)),
            pl.BlockSpec((1, D_MODEL), lambda i, e: (0, 0)),
        ],
        out_specs=pl.BlockSpec((t, D_MODEL), lambda i, e: (i, 0)),
        out_shape=jax.ShapeDtypeStruct((n, D_MODEL), F32),
        scratch_shapes=[pltpu.VMEM((D_MODEL, t), F32)],
        compiler_params=_params("parallel", "arbitrary"),
        name="peer_experts",
    )(ht, n0, w0, r1, e1, u_bf, vt_bf, x2, final_w)


def _pad_lanes(v, width):
    return jnp.pad(v, ((0, 0), (0, width - v.shape[1])))


def kernel(x, norm1_w, w_in, ssm_conv_w, ssm_conv_b, ssm_dt_bias, ssm_A_log, ssm_D, ssm_norm_w,
           sc_conv_w, w_out, norm2_w, peer_w_q, peer_sub_keys, peer_u, peer_v, final_norm_w):
    batch, seq, d = x.shape
    depth = w_in.shape[0]
    n = batch * seq
    x2 = x.reshape(n, d)

    cuts = [0]
    for width in (SB_WIDTH, SB_WIDTH, SB_WIDTH, SSM_WIDTH, SSM_CONV_DIM, SSM_HEADS,
                  SC_WIDTH, SC_WIDTH, SC_WIDTH):
        cuts.append(cuts[-1] + width)
    ident = jnp.arange(SB_BLOCK)
    upper = (ident[:, None] > ident[None, :]).astype(BF16)
    identm = jnp.arange(MIX_BLOCK)
    tri = (identm[None, :] <= identm[:, None]).astype(BF16)

    for layer in range(depth):
        w = w_in[layer]
        w_cat = jnp.concatenate([
            w[:, cuts[3]:cuts[5]],
            _pad_lanes(w[:, cuts[5]:cuts[6]], DT_PAD),
            w[:, cuts[6]:cuts[9]],
            w[:, cuts[0]:cuts[3]],
        ], axis=1).astype(BF16)
        rest, qkv = _in_proj(x2, norm1_w[layer][None, :], w_cat)

        ysb = _sb_attention(qkv, batch, upper)

        x2 = _mixer(
            rest, ysb, x2, batch,
            ssm_conv_w[layer], ssm_conv_b[layer][None, :],
            _pad_lanes(ssm_dt_bias[layer][None, :], DT_PAD),
            _pad_lanes(ssm_A_log[layer][None, :], DT_PAD),
            jnp.repeat(ssm_D[layer], SSM_HEAD_DIM)[None, :],
            ssm_norm_w[layer][None, :], sc_conv_w[layer],
            w_out[layer].astype(BF16), tri)

        ht, n0, w0, r1, e1 = _route(
            x2, norm2_w[layer][None, :], peer_w_q[layer].T, peer_sub_keys[layer])
        x2 = _experts(ht, n0, w0, r1, e1,
                      (peer_u[layer] * INV_SQRT2).astype(BF16), peer_v[layer].T.astype(BF16),
                      x2, final_norm_w[None, :], layer == depth - 1)
    return x2.reshape(batch, seq, d)
```

```python
import functools
import math

import jax
import jax.numpy as jnp
from jax import lax
from jax.experimental import pallas as pl
from jax.experimental.pallas import tpu as pltpu

F32 = jnp.float32
BF16 = jnp.bfloat16

EPS = 1e-6
D_MODEL = 1024
SB_HEADS = 4
SB_HEAD_DIM = 64
SB_WIDTH = SB_HEADS * SB_HEAD_DIM
SSM_HEADS = 8
SSM_HEAD_DIM = 64
SSM_WIDTH = SSM_HEADS * SSM_HEAD_DIM
SSM_GROUPS = 2
SSM_STATE = 64
SSM_CONV = 4
SSM_CONV_DIM = SSM_WIDTH + 2 * SSM_GROUPS * SSM_STATE
SC_WIDTH = 256
SC_CONV = 3
PEER_HEADS = 4
PEER_KEY_DIM = 256
PEER_NKEYS = 128
PEER_TOPK = 16

LANES = 128
SUBLANES = 8
DT_PAD = LANES
REST_WIDTH = SSM_WIDTH + SSM_CONV_DIM + DT_PAD + 3 * SC_WIDTH
QKV_WIDTH = 3 * SB_WIDTH
VMEM_LIMIT_MAX = 56 * 1024 * 1024
VMEM_SPILL_HEADROOM = 8 * 1024 * 1024

IN_PROJ_ROWS = 512
SB_BLOCK = 256
MIX_BLOCK = 256
ROUTE_BLOCK = 256
PEER_TOKENS = 512
PEER_EXPERTS_PER_STEP = 2048
SB_SKIP_LOG = -120.0
INV_SQRT2 = 1.0 / math.sqrt(2.0)
NEG_INF = float("-inf")


def _rmsnorm(x, w):
    return x * lax.rsqrt(jnp.mean(x * x, axis=-1, keepdims=True) + EPS) * w


def _softplus(x):
    return jnp.maximum(x, 0.0) + jnp.log(1.0 + jnp.exp(-jnp.abs(x)))


def _silu(x):
    return x * (1.0 / (1.0 + jnp.exp(-x)))


def _nbytes(shape, dtype):
    return math.prod(shape) * jnp.dtype(dtype).itemsize


def _params(semantics, pipelined_bytes, resident_bytes):
    limit = min(2 * pipelined_bytes + resident_bytes + VMEM_SPILL_HEADROOM, VMEM_LIMIT_MAX)
    return pltpu.CompilerParams(dimension_semantics=semantics, vmem_limit_bytes=limit)


def _in_proj_kernel(x_ref, nw_ref, w_ref, rest_ref, qkv_ref):
    h = _rmsnorm(x_ref[...], nw_ref[...]).astype(BF16)
    p = jnp.dot(h, w_ref[...], preferred_element_type=F32)
    rest_ref[...] = p[:, :REST_WIDTH]
    qkv_ref[...] = p[:, REST_WIDTH:].astype(BF16)


def _in_proj(x2, norm_w, w_cat):
    n = x2.shape[0]
    tm = IN_PROJ_ROWS
    width = w_cat.shape[1]
    return pl.pallas_call(
        _in_proj_kernel,
        grid=(n // tm,),
        in_specs=[
            pl.BlockSpec((tm, D_MODEL), lambda i: (i, 0)),
            pl.BlockSpec((1, D_MODEL), lambda i: (0, 0)),
            pl.BlockSpec((D_MODEL, width), lambda i: (0, 0)),
        ],
        out_specs=[
            pl.BlockSpec((tm, REST_WIDTH), lambda i: (i, 0)),
            pl.BlockSpec((tm, QKV_WIDTH), lambda i: (i, 0)),
        ],
        out_shape=[
            jax.ShapeDtypeStruct((n, REST_WIDTH), F32),
            jax.ShapeDtypeStruct((n, QKV_WIDTH), BF16),
        ],
        compiler_params=_params(
            ("parallel",),
            _nbytes((tm, D_MODEL), F32) + _nbytes((D_MODEL, width), BF16)
            + _nbytes((tm, REST_WIDTH), F32) + _nbytes((tm, QKV_WIDTH), BF16),
            _nbytes((tm, width), F32) + _nbytes((tm, D_MODEL), F32)),
        name="in_proj",
    )(x2, norm_w, w_cat)


def _sb_attn_kernel(q_ref, k_ref, v_ref, upper_ref, o_ref):
    t = q_ref.shape[0]
    i = pl.program_id(1)
    upper = upper_ref[...]
    row = lax.broadcasted_iota(jnp.int32, (t, t), 0)
    col = lax.broadcasted_iota(jnp.int32, (t, t), 1)
    lane_head = lax.broadcasted_iota(jnp.int32, (1, SB_WIDTH), 1) // SB_HEAD_DIM
    head_mask = [jnp.where(lane_head == h, 1.0, 0.0).astype(BF16) for h in range(SB_HEADS)]
    qs = q_ref[...] * (SB_HEAD_DIM ** -0.5)
    q_heads = [qs * head_mask[h] for h in range(SB_HEADS)]

    def block(j, accs, out, diagonal):
        start = pl.multiple_of(j * t, t)
        kb = k_ref[pl.ds(start, t), :]
        vb = v_ref[pl.ds(start, t), :]
        new_accs = []
        for h in range(SB_HEADS):
            z = lax.dot_general(q_heads[h], kb, (((1,), (1,)), ((), ())), preferred_element_type=F32)
            log1m = jnp.minimum(-z, 0.0) - jnp.log(1.0 + jnp.exp(-jnp.abs(z)))
            if diagonal:
                mask = col < row
                log1m_m = jnp.where(mask, log1m, 0.0)
            else:
                log1m_m = log1m
            rev = jnp.dot(log1m_m.astype(BF16), upper, preferred_element_type=F32)
            w = jnp.exp(log1m + z + rev + accs[h])
            if diagonal:
                w = jnp.where(mask, w, 0.0)
            out = out + jnp.dot(w.astype(BF16), vb * head_mask[h], preferred_element_type=F32)
            new_accs.append(accs[h] + rev[:, 0:1] + log1m_m[:, 0:1])
        return tuple(new_accs), out

    accs0 = tuple(jnp.zeros((t, 1), F32) for _ in range(SB_HEADS))
    out0 = jnp.zeros((t, SB_WIDTH), F32)
    accs0, out0 = block(i, accs0, out0, True)
    accs1, out1 = block(jnp.maximum(i - 1, 0), accs0, jnp.zeros((t, SB_WIDTH), F32), False)
    has_prev = i > 0
    accs0 = tuple(jnp.where(has_prev, a1, a0) for a0, a1 in zip(accs0, accs1))
    out0 = out0 + jnp.where(has_prev, out1, 0.0)

    def cond(carry):
        j, accs, _ = carry
        worst = functools.reduce(jnp.maximum, accs)
        return jnp.logical_and(j >= 0, jnp.max(worst) > SB_SKIP_LOG)

    def body(carry):
        j, accs, out = carry
        accs, out = block(j, accs, out, False)
        return j - 1, accs, out

    _, _, out = lax.while_loop(cond, body, (i - 2, accs0, out0))
    o_ref[...] = out


def _sb_attention(qkv, batch, upper):
    n = qkv.shape[0]
    s = n // batch
    t = SB_BLOCK
    nq = s // t
    return pl.pallas_call(
        _sb_attn_kernel,
        grid=(batch, nq),
        in_specs=[
            pl.BlockSpec((t, SB_WIDTH), lambda b, i: (b * nq + i, 0)),
            pl.BlockSpec((s, SB_WIDTH), lambda b, i: (b, 1)),
            pl.BlockSpec((s, SB_WIDTH), lambda b, i: (b, 2)),
            pl.BlockSpec((t, t), lambda b, i: (0, 0)),
        ],
        out_specs=pl.BlockSpec((t, SB_WIDTH), lambda b, i: (b * nq + i, 0)),
        out_shape=jax.ShapeDtypeStruct((n, SB_WIDTH), F32),
        compiler_params=_params(
            ("parallel", "arbitrary"),
            2 * _nbytes((s, SB_WIDTH), BF16) + _nbytes((t, SB_WIDTH), BF16) + _nbytes((t, t), BF16)
            + _nbytes((t, SB_WIDTH), F32),
            2 * SB_HEADS * _nbytes((t, t), F32)),
        name="sb_attention",
    )(qkv, qkv, qkv, upper)


def _mixer_kernel(rest_ref, ysb_ref, x_ref, convw_ref, convb_ref, dtb_ref, alog_ref, dexp_ref,
                  normw_ref, scw_ref, wout_ref, tri_ref, o_ref,
                  xbc_ext, sc_ext, state_ref):
    t = rest_ref.shape[0]
    sblk = pl.program_id(1)
    halo = SUBLANES

    @pl.when(sblk == 0)
    def _():
        xbc_ext[0:halo, :] = jnp.zeros((halo, SSM_CONV_DIM), F32)
        sc_ext[0:halo, :] = jnp.zeros((halo, SC_WIDTH), F32)
        state_ref[...] = jnp.zeros_like(state_ref)

    o_z = 0
    o_xbc = SSM_WIDTH
    o_dt = o_xbc + SSM_CONV_DIM
    o_sc = o_dt + DT_PAD
    z = rest_ref[:, o_z:o_z + SSM_WIDTH]
    dt_raw = rest_ref[:, o_dt:o_dt + DT_PAD]
    sc_b = rest_ref[:, o_sc:o_sc + SC_WIDTH]
    sc_c = rest_ref[:, o_sc + SC_WIDTH:o_sc + 2 * SC_WIDTH]
    sc_v = rest_ref[:, o_sc + 2 * SC_WIDTH:o_sc + 3 * SC_WIDTH]

    xbc_ext[halo:halo + t, :] = rest_ref[:, o_xbc:o_xbc + SSM_CONV_DIM]
    conv = convb_ref[...]
    for kk in range(SSM_CONV):
        shift = SSM_CONV - 1 - kk
        conv = conv + convw_ref[kk:kk + 1, :] * xbc_ext[halo - shift:halo - shift + t, :]
    tail = xbc_ext[t:t + halo, :]
    xbc_ext[0:halo, :] = tail
    xbc = _silu(conv)
    xs = xbc[:, :SSM_WIDTH]

    sc_ext[halo:halo + t, :] = sc_c * sc_v
    sconv = jnp.zeros((t, SC_WIDTH), F32)
    for kk in range(SC_CONV):
        shift = SC_CONV - 1 - kk
        sconv = sconv + scw_ref[kk:kk + 1, :] * sc_ext[halo - shift:halo - shift + t, :]
    sc_tail = sc_ext[t:t + halo, :]
    sc_ext[0:halo, :] = sc_tail
    y_sc = sc_b * sconv

    dt = _softplus(dt_raw + dtb_ref[...])
    a = dt * (-jnp.exp(alog_ref[...]))
    a_hi = a.astype(BF16)
    a_lo = (a - a_hi.astype(F32)).astype(BF16)
    tri = tri_ref[...]
    a_cs = (jnp.dot(tri, a_hi, preferred_element_type=F32)
            + jnp.dot(tri, a_lo, preferred_element_type=F32))
    a_cs_t = a_cs.T
    a_end = a_cs[t - 1:t, :]
    row = lax.broadcasted_iota(jnp.int32, (t, t), 0)
    col = lax.broadcasted_iota(jnp.int32, (t, t), 1)
    causal = row >= col

    def lanes_per_head(cols, h0, nh):
        return jnp.concatenate(
            [jnp.broadcast_to(cols[:, h:h + 1], (cols.shape[0], SSM_HEAD_DIM)) for h in range(h0, h0 + nh)],
            axis=1)

    hpg = SSM_HEADS // SSM_GROUPS
    gw = hpg * SSM_HEAD_DIM
    lane_head = lax.broadcasted_iota(jnp.int32, (t, gw), 1) // SSM_HEAD_DIM
    y_groups = []
    for g in range(SSM_GROUPS):
        b_g = xbc[:, SSM_WIDTH + g * SSM_STATE:SSM_WIDTH + (g + 1) * SSM_STATE]
        c_g = xbc[:, SSM_WIDTH + SSM_GROUPS * SSM_STATE + g * SSM_STATE:
                  SSM_WIDTH + SSM_GROUPS * SSM_STATE + (g + 1) * SSM_STATE]
        b_bf = b_g.astype(BF16)
        c_bf = c_g.astype(BF16)
        xs_g = xs[:, g * gw:(g + 1) * gw]
        dt_g = lanes_per_head(dt, g * hpg, hpg)
        acs_g = lanes_per_head(a_cs, g * hpg, hpg)
        aend_g = lanes_per_head(a_end, g * hpg, hpg)
        xd_g = xs_g * dt_g
        xd_bf = xd_g.astype(BF16)
        cb = lax.dot_general(c_bf, b_bf, (((1,), (1,)), ((), ())), preferred_element_type=F32)
        y_diag = jnp.zeros((t, gw), F32)
        for hh in range(hpg):
            h = g * hpg + hh
            seg = a_cs[:, h:h + 1] - a_cs_t[h:h + 1, :]
            lmat = jnp.exp(jnp.where(causal, seg, NEG_INF))
            p_h = jnp.dot((cb * lmat).astype(BF16), xd_bf, preferred_element_type=F32)
            y_diag = jnp.where(lane_head == hh, p_h, y_diag)
        state = state_ref[g]
        y_off = jnp.dot(c_bf, state.astype(BF16), preferred_element_type=F32) * jnp.exp(acs_g)
        xdd = (xd_g * jnp.exp(aend_g - acs_g)).astype(BF16)
        contrib = lax.dot_general(b_bf, xdd, (((0,), (0,)), ((), ())), preferred_element_type=F32)
        state_ref[g] = state * jnp.exp(aend_g) + contrib
        y_groups.append(y_diag + y_off + dexp_ref[:, g * gw:(g + 1) * gw] * xs_g)
    y = jnp.concatenate(y_groups, axis=1) * _silu(z)
    y_ssm = _rmsnorm(y, normw_ref[...])

    o0 = SB_WIDTH
    o1 = o0 + SSM_WIDTH
    out = x_ref[...]
    out = out + jnp.dot(ysb_ref[...].astype(BF16), wout_ref[0:o0, :], preferred_element_type=F32)
    out = out + jnp.dot(y_ssm.astype(BF16), wout_ref[o0:o1, :], preferred_element_type=F32)
    out = out + jnp.dot(y_sc.astype(BF16), wout_ref[o1:o1 + SC_WIDTH, :], preferred_element_type=F32)
    o_ref[...] = out


def _mixer(rest, ysb, x2, batch, conv_w, conv_b, dt_bias, a_log, d_exp, norm_w, sc_w, w_out, tri):
    n = x2.shape[0]
    t = MIX_BLOCK
    spb = n // batch // t
    row_map = lambda b, s: (b * spb + s, 0)
    const = lambda b, s: (0, 0)
    return pl.pallas_call(
        _mixer_kernel,
        grid=(batch, spb),
        in_specs=[
            pl.BlockSpec((t, REST_WIDTH), row_map),
            pl.BlockSpec((t, SB_WIDTH), row_map),
            pl.BlockSpec((t, D_MODEL), row_map),
            pl.BlockSpec((SSM_CONV, SSM_CONV_DIM), const),
            pl.BlockSpec((1, SSM_CONV_DIM), const),
            pl.BlockSpec((1, DT_PAD), const),
            pl.BlockSpec((1, DT_PAD), const),
            pl.BlockSpec((1, SSM_WIDTH), const),
            pl.BlockSpec((1, SSM_WIDTH), const),
            pl.BlockSpec((SC_CONV, SC_WIDTH), const),
            pl.BlockSpec((D_MODEL, D_MODEL), const),
            pl.BlockSpec((t, t), const),
        ],
        out_specs=pl.BlockSpec((t, D_MODEL), row_map),
        out_shape=jax.ShapeDtypeStruct((n, D_MODEL), F32),
        scratch_shapes=[
            pltpu.VMEM((t + SUBLANES, SSM_CONV_DIM), F32),
            pltpu.VMEM((t + SUBLANES, SC_WIDTH), F32),
            pltpu.VMEM((SSM_GROUPS, SSM_STATE, (SSM_HEADS // SSM_GROUPS) * SSM_HEAD_DIM), F32),
        ],
        compiler_params=_params(
            ("arbitrary", "arbitrary"),
            _nbytes((t, REST_WIDTH), F32) + _nbytes((t, SB_WIDTH), F32) + 2 * _nbytes((t, D_MODEL), F32)
            + _nbytes((D_MODEL, D_MODEL), BF16) + _nbytes((t, t), BF16),
            _nbytes((t + SUBLANES, SSM_CONV_DIM + SC_WIDTH), F32) + 4 * _nbytes((t, t), F32)
            + _nbytes((t, REST_WIDTH), F32)),
        name="mixer",
    )(rest, ysb, x2, conv_w, conv_b, dt_bias, a_log, d_exp, norm_w, sc_w, w_out, tri)


def _sort16_pairs():
    def merge(lo, hi, r):
        step = r * 2
        if step < hi - lo:
            yield from merge(lo, hi, step)
            yield from merge(lo + r, hi, step)
            for i in range(lo + r, hi - r, step):
                yield (i, i + r)
        else:
            yield (lo, lo + r)

    def sort(lo, hi):
        if hi - lo >= 1:
            mid = lo + (hi - lo) // 2
            yield from sort(lo, mid)
            yield from sort(mid + 1, hi)
            yield from merge(lo, hi, 1)

    return tuple(sort(0, PEER_TOPK - 1))


_SORT16 = _sort16_pairs()


def _exchange(v, i, j):
    v[i], v[j] = jnp.maximum(v[i], v[j]), jnp.minimum(v[i], v[j])


def _merge_sublane_lists(v):
    for shift in (4, 2, 1):
        r = [pltpu.roll(x, shift, axis=0) for x in v]
        v = [jnp.maximum(v[i], r[PEER_TOPK - 1 - i]) for i in range(PEER_TOPK)]
        d = PEER_TOPK // 2
        while d >= 1:
            for i in range(PEER_TOPK):
                if not i & d:
                    _exchange(v, i, i + d)
            d //= 2
    return v


def _top16_keys(scores):
    v = [scores[k * SUBLANES:(k + 1) * SUBLANES, :] for k in range(PEER_NKEYS // SUBLANES)]
    for i, j in _SORT16:
        _exchange(v, i, j)
    return _merge_sublane_lists(v)


def _top16_pairs(top0, top1):
    t = top0[0].shape[1]
    sub = lax.broadcasted_iota(jnp.int32, (SUBLANES, t), 0)
    lengths = (16, 8, 5, 4, 12, 4, 1, 0)
    length = jnp.zeros((SUBLANES, t), jnp.int32)
    for s_, n_ in enumerate(lengths):
        length = jnp.where(sub == s_, n_, length)
    row_part = jnp.where(sub == 0, top0[0], jnp.where(sub == 1, top0[1],
                                                     jnp.where(sub == 2, top0[2], top0[3])))
    col_part = jnp.where(sub == 4, top1[0], jnp.where(sub == 5, top1[1], top1[2]))
    lists = []
    for r in range(PEER_TOPK):
        a = jnp.where(sub < 4, row_part, top0[min(4 + r, PEER_TOPK - 1)])
        b = jnp.where(sub < 4, top1[r], col_part)
        lists.append(jnp.where(r < length, a + b, NEG_INF))
    return _merge_sublane_lists(lists)


def _split_bf16(x):
    hi = x.astype(BF16)
    return hi, (x - hi.astype(F32)).astype(BF16)


def _dot3(a_hi, a_lo, b_hi, b_lo):
    return (jnp.dot(a_hi, b_hi, preferred_element_type=F32)
            + jnp.dot(a_hi, b_lo, preferred_element_type=F32)
            + jnp.dot(a_lo, b_hi, preferred_element_type=F32))


def _fold_keys_kernel(keys_ref, wqt_ref, hi_ref, lo_ref):
    k_hi, k_lo = _split_bf16(keys_ref[...])
    w_hi, w_lo = _split_bf16(wqt_ref[...])
    hi_ref[...], lo_ref[...] = _split_bf16(_dot3(k_hi, k_lo, w_hi, w_lo))


def _fold_keys(wq_t, keys):
    half = PEER_KEY_DIM // 2
    nblk = wq_t.shape[0] // half
    out = jax.ShapeDtypeStruct((nblk * PEER_NKEYS, D_MODEL), BF16)
    out_spec = pl.BlockSpec((PEER_NKEYS, D_MODEL), lambda i: (i, 0))
    return pl.pallas_call(
        _fold_keys_kernel,
        grid=(nblk,),
        in_specs=[
            pl.BlockSpec((None, PEER_NKEYS, half), lambda i: (i % 2, 0, 0)),
            pl.BlockSpec((half, D_MODEL), lambda i: (i, 0)),
        ],
        out_specs=[out_spec, out_spec],
        out_shape=[out, out],
        compiler_params=_params(
            ("parallel",),
            _nbytes((PEER_NKEYS, half), F32) + _nbytes((half, D_MODEL), F32)
            + 2 * _nbytes((PEER_NKEYS, D_MODEL), BF16),
            2 * _nbytes((PEER_NKEYS, D_MODEL), F32)),
        name="peer_fold_keys",
    )(keys, wq_t)


def _route_kernel(x_ref, nw_ref, wk_hi_ref, wk_lo_ref, ht_ref, n0_ref, w0_ref, r1_ref, e1_ref):
    h = _rmsnorm(x_ref[...], nw_ref[...])
    ht_hi, ht_lo = _split_bf16(h.T)
    ht_ref[...] = ht_hi
    scores = _dot3(wk_hi_ref[...], wk_lo_ref[...], ht_hi, ht_lo)
    for hh in range(PEER_HEADS):
        s0 = scores[(2 * hh) * PEER_NKEYS:(2 * hh + 1) * PEER_NKEYS, :]
        s1 = scores[(2 * hh + 1) * PEER_NKEYS:(2 * hh + 2) * PEER_NKEYS, :]
        for l0 in range(0, scores.shape[1], LANES):
            ln = slice(l0, l0 + LANES)
            _route_lane_tile(s0[:, ln], s1[:, ln], hh, ln, n0_ref, w0_ref, r1_ref, e1_ref)


def _count_leading(t, pred):
    c8 = pred(t[7])
    c4 = pred(jnp.where(c8, t[11], t[3]))
    c2 = pred(jnp.where(c8, jnp.where(c4, t[13], t[9]), jnp.where(c4, t[5], t[1])))
    c1 = pred(jnp.where(c8,
                        jnp.where(c4, jnp.where(c2, t[14], t[12]), jnp.where(c2, t[10], t[8])),
                        jnp.where(c4, jnp.where(c2, t[6], t[4]), jnp.where(c2, t[2], t[0]))))
    count = (jnp.where(c8, 8.0, 0.0) + jnp.where(c4, 4.0, 0.0)
             + jnp.where(c2, 2.0, 0.0) + jnp.where(c1, 1.0, 0.0))
    return count + jnp.where(pred(t[15]), 1.0, 0.0)


def _route_lane_tile(s0, s1, hh, ln, n0_ref, w0_ref, r1_ref, e1_ref):
    top0 = _top16_keys(s0)
    top1 = _top16_keys(s1)
    best = _top16_pairs(top0, top1)
    z = jnp.exp(best[0] - best[0])
    for r in range(1, PEER_TOPK):
        z = z + jnp.exp(best[r] - best[0])
    tau = best[PEER_TOPK - 1]
    scale0 = INV_SQRT2 / z
    r1_rows, e1_rows = [], []
    for k in range(PEER_NKEYS // SUBLANES):
        rows = slice(k * SUBLANES, (k + 1) * SUBLANES)
        s0k = s0[rows, :]
        s1k = s1[rows, :]
        n0_ref[hh, rows, ln] = _count_leading(top1, lambda m: s0k + m >= tau)
        w0_ref[hh, rows, ln] = jnp.exp(s0k - top0[0]) * scale0
        r1_rows.append(_count_leading(top1, lambda m: m > s1k))
        e1_rows.append(jnp.exp(s1k - top1[0]))
    r1_ref[hh, :, ln] = jnp.concatenate(r1_rows, axis=0).astype(BF16)
    e1_ref[hh, :, ln] = jnp.concatenate(e1_rows, axis=0).astype(BF16)


def _route(x2, norm_w, wq_t, keys):
    n = x2.shape[0]
    t = ROUTE_BLOCK
    hk = (PEER_HEADS, PEER_NKEYS, n)
    hk_spec = pl.BlockSpec((PEER_HEADS, PEER_NKEYS, t), lambda i: (0, 0, i))
    wk_hi, wk_lo = _fold_keys(wq_t, keys)
    wk_spec = pl.BlockSpec(wk_hi.shape, lambda i: (0, 0))
    return pl.pallas_call(
        _route_kernel,
        grid=(n // t,),
        in_specs=[
            pl.BlockSpec((t, D_MODEL), lambda i: (i, 0)),
            pl.BlockSpec((1, D_MODEL), lambda i: (0, 0)),
            wk_spec, wk_spec,
        ],
        out_specs=[
            pl.BlockSpec((D_MODEL, t), lambda i: (0, i)),
            hk_spec, hk_spec, hk_spec, hk_spec,
        ],
        out_shape=[
            jax.ShapeDtypeStruct((D_MODEL, n), BF16),
            jax.ShapeDtypeStruct(hk, F32), jax.ShapeDtypeStruct(hk, F32),
            jax.ShapeDtypeStruct(hk, BF16), jax.ShapeDtypeStruct(hk, BF16),
        ],
        compiler_params=_params(
            ("parallel",),
            _nbytes((t, D_MODEL), F32) + 2 * _nbytes(wk_hi.shape, BF16) + _nbytes((D_MODEL, t), BF16)
            + 2 * _nbytes((PEER_HEADS, PEER_NKEYS, t), F32) + 2 * _nbytes((PEER_HEADS, PEER_NKEYS, t), BF16),
            3 * _nbytes((D_MODEL, t), F32)),
        name="peer_route",
    )(x2, norm_w, wk_hi, wk_lo)


def _bf16_rows(row, n):
    packed = 2 * SUBLANES
    tile = jnp.broadcast_to(row, (packed, row.shape[1])).astype(BF16)
    return jnp.concatenate([tile] * (n // packed), axis=0)


def _expert_kernel(ht_ref, n0_ref, w0_ref, r1_ref, e1_ref, u_ref, vt_ref, x_ref, fw_ref,
                   o_ref, acc_ref, *, final_norm):
    e = pl.program_id(1)
    eb = u_ref.shape[0]
    t = ht_ref.shape[1]

    @pl.when(e == 0)
    def _():
        acc_ref[...] = jnp.zeros_like(acc_ref)

    def gated(y, sub):
        gate = jnp.zeros((PEER_NKEYS, t), BF16)
        for hh in range(PEER_HEADS):
            n0 = _bf16_rows(n0_ref[hh, sub:sub + 1, :], PEER_NKEYS)
            w0 = _bf16_rows(w0_ref[hh, sub:sub + 1, :], PEER_NKEYS)
            gate = gate + jnp.where(r1_ref[hh] < n0, e1_ref[hh], 0.0) * w0
        return gate * (y * (1.0 + lax.erf(y))).astype(BF16)

    half = eb // 2
    nsub = half // PEER_NKEYS
    ht = ht_ref[...]
    y_a = jnp.dot(u_ref[0:half, :], ht, preferred_element_type=F32)
    rows_a = [gated(y_a[s * PEER_NKEYS:(s + 1) * PEER_NKEYS, :], s) for s in range(nsub)]
    bits = pltpu.bitcast(rows_a[0][0:2 * SUBLANES, 0:LANES], jnp.uint32)
    sixteen = jnp.uint32(16)
    zero = pltpu.bitcast(lax.shift_right_logical(lax.shift_right_logical(bits, sixteen), sixteen), BF16)
    ht_b = ht + jnp.tile(zero, (ht.shape[0] // zero.shape[0], t // LANES))
    y_b = jnp.dot(u_ref[half:eb, :], ht_b, preferred_element_type=F32)
    rows_b = [gated(y_b[s * PEER_NKEYS:(s + 1) * PEER_NKEYS, :], nsub + s) for s in range(nsub)]
    weighted = jnp.concatenate(rows_a + rows_b, axis=0)
    acc_ref[...] += jnp.dot(vt_ref[...], weighted, preferred_element_type=F32)

    @pl.when(e == pl.num_programs(1) - 1)
    def _():
        out = x_ref[...] + acc_ref[...].T
        if final_norm:
            out = _rmsnorm(out, fw_ref[...])
        o_ref[...] = out


def _experts(ht, n0, w0, r1, e1, u_bf, vt_bf, x2, final_w, final_norm):
    n = x2.shape[0]
    t = PEER_TOKENS
    eb = PEER_EXPERTS_PER_STEP
    ne = u_bf.shape[0] // eb
    hk_spec = pl.BlockSpec((PEER_HEADS, PEER_NKEYS, t), lambda i, e: (0, 0, i))
    ha_spec = pl.BlockSpec((PEER_HEADS, eb // PEER_NKEYS, t), lambda i, e: (0, e, i))
    return pl.pallas_call(
        functools.partial(_expert_kernel, final_norm=final_norm),
        grid=(n // t, ne),
        in_specs=[
            pl.BlockSpec((D_MODEL, t), lambda i, e: (0, i)),
            ha_spec, ha_spec, hk_spec, hk_spec,
            pl.BlockSpec((eb, D_MODEL), lambda i, e: (e, 0)),
            pl.BlockSpec((D_MODEL, eb), lambda i, e: (0, e)),
            pl.BlockSpec((t, D_MODEL), lambda i, e: (i, 0)),
            pl.BlockSpec((1, D_MODEL), lambda i, e: (0, 0)),
        ],
        out_specs=pl.BlockSpec((t, D_MODEL), lambda i, e: (i, 0)),
        out_shape=jax.ShapeDtypeStruct((n, D_MODEL), F32),
        scratch_shapes=[pltpu.VMEM((D_MODEL, t), F32)],
        compiler_params=_params(
            ("parallel", "arbitrary"),
            _nbytes((D_MODEL, t), BF16) + 2 * _nbytes((PEER_HEADS, eb // PEER_NKEYS, t), F32)
            + 2 * _nbytes((PEER_HEADS, PEER_NKEYS, t), BF16) + 2 * _nbytes((eb, D_MODEL), BF16)
            + 2 * _nbytes((t, D_MODEL), F32),
            _nbytes((D_MODEL, t), F32) + _nbytes((eb, t), F32) + _nbytes((eb, t), BF16)),
        name="peer_experts",
    )(ht, n0, w0, r1, e1, u_bf, vt_bf, x2, final_w)


def _pad_lanes(v, width):
    return jnp.pad(v, ((0, 0), (0, width - v.shape[1])))


def kernel(x, norm1_w, w_in, ssm_conv_w, ssm_conv_b, ssm_dt_bias, ssm_A_log, ssm_D, ssm_norm_w,
           sc_conv_w, w_out, norm2_w, peer_w_q, peer_sub_keys, peer_u, peer_v, final_norm_w):
    batch, seq, d = x.shape
    depth = w_in.shape[0]
    n = batch * seq
    x2 = x.reshape(n, d)

    cuts = [0]
    for width in (SB_WIDTH, SB_WIDTH, SB_WIDTH, SSM_WIDTH, SSM_CONV_DIM, SSM_HEADS,
                  SC_WIDTH, SC_WIDTH, SC_WIDTH):
        cuts.append(cuts[-1] + width)
    ident = jnp.arange(SB_BLOCK)
    upper = (ident[:, None] > ident[None, :]).astype(BF16)
    identm = jnp.arange(MIX_BLOCK)
    tri = (identm[None, :] <= identm[:, None]).astype(BF16)

    for layer in range(depth):
        w = w_in[layer]
        w_cat = jnp.concatenate([
            w[:, cuts[3]:cuts[5]],
            _pad_lanes(w[:, cuts[5]:cuts[6]], DT_PAD),
            w[:, cuts[6]:cuts[9]],
            w[:, cuts[0]:cuts[3]],
        ], axis=1).astype(BF16)
        rest, qkv = _in_proj(x2, norm1_w[layer][None, :], w_cat)

        ysb = _sb_attention(qkv, batch, upper)

        x2 = _mixer(
            rest, ysb, x2, batch,
            ssm_conv_w[layer], ssm_conv_b[layer][None, :],
            _pad_lanes(ssm_dt_bias[layer][None, :], DT_PAD),
            _pad_lanes(ssm_A_log[layer][None, :], DT_PAD),
            jnp.repeat(ssm_D[layer], SSM_HEAD_DIM)[None, :],
            ssm_norm_w[layer][None, :], sc_conv_w[layer],
            w_out[layer].astype(BF16), tri)

        ht, n0, w0, r1, e1 = _route(
            x2, norm2_w[layer][None, :], peer_w_q[layer].T, peer_sub_keys[layer])
        x2 = _experts(ht, n0, w0, r1, e1,
                      (peer_u[layer] * INV_SQRT2).astype(BF16), peer_v[layer].T.astype(BF16),
                      x2, final_norm_w[None, :], layer == depth - 1)
    return x2.reshape(batch, seq, d)
```

```python
import functools
import math

import jax
import jax.numpy as jnp
from jax import lax
from jax.experimental import pallas as pl
from jax.experimental.pallas import tpu as pltpu

F32 = jnp.float32
BF16 = jnp.bfloat16

EPS = 1e-6
D_MODEL = 1024
SB_HEADS = 4
SB_HEAD_DIM = 64
SB_WIDTH = SB_HEADS * SB_HEAD_DIM
SSM_HEADS = 8
SSM_HEAD_DIM = 64
SSM_WIDTH = SSM_HEADS * SSM_HEAD_DIM
SSM_GROUPS = 2
SSM_STATE = 64
SSM_CONV = 4
SSM_CONV_DIM = SSM_WIDTH + 2 * SSM_GROUPS * SSM_STATE
SC_WIDTH = 256
SC_CONV = 3
PEER_HEADS = 4
PEER_KEY_DIM = 256
PEER_NKEYS = 128
PEER_TOPK = 16

LANES = 128
SUBLANES = 8
DT_PAD = LANES
REST_WIDTH = SSM_WIDTH + SSM_CONV_DIM + DT_PAD + 3 * SC_WIDTH
QKV_WIDTH = 3 * SB_WIDTH
VMEM_LIMIT_MAX = 56 * 1024 * 1024
VMEM_SPILL_HEADROOM = 8 * 1024 * 1024

IN_PROJ_ROWS = 512
SB_BLOCK = 256
SB_QUERY_BLOCKS = 4
MIX_BLOCK = 256
ROUTE_BLOCK = 256
PEER_TOKENS = 512
PEER_EXPERTS_PER_STEP = 2048
SB_SKIP_LOG = -120.0
INV_SQRT2 = 1.0 / math.sqrt(2.0)
NEG_INF = float("-inf")


def _rmsnorm(x, w):
    return x * lax.rsqrt(jnp.mean(x * x, axis=-1, keepdims=True) + EPS) * w


def _softplus(x):
    return jnp.maximum(x, 0.0) + jnp.log(1.0 + jnp.exp(-jnp.abs(x)))


def _silu(x):
    return x * (1.0 / (1.0 + jnp.exp(-x)))


def _nbytes(shape, dtype):
    return math.prod(shape) * jnp.dtype(dtype).itemsize


def _params(semantics, pipelined_bytes, resident_bytes):
    limit = min(2 * pipelined_bytes + resident_bytes + VMEM_SPILL_HEADROOM, VMEM_LIMIT_MAX)
    return pltpu.CompilerParams(dimension_semantics=semantics, vmem_limit_bytes=limit)


def _in_proj_kernel(x_ref, nw_ref, w_ref, rest_ref, qkv_ref):
    h = _rmsnorm(x_ref[...], nw_ref[...]).astype(BF16)
    p = jnp.dot(h, w_ref[...], preferred_element_type=F32)
    rest_ref[...] = p[:, :REST_WIDTH]
    qkv_ref[...] = p[:, REST_WIDTH:].astype(BF16)


def _in_proj(x2, norm_w, w_cat):
    n = x2.shape[0]
    tm = IN_PROJ_ROWS
    width = w_cat.shape[1]
    return pl.pallas_call(
        _in_proj_kernel,
        grid=(n // tm,),
        in_specs=[
            pl.BlockSpec((tm, D_MODEL), lambda i: (i, 0)),
            pl.BlockSpec((1, D_MODEL), lambda i: (0, 0)),
            pl.BlockSpec((D_MODEL, width), lambda i: (0, 0)),
        ],
        out_specs=[
            pl.BlockSpec((tm, REST_WIDTH), lambda i: (i, 0)),
            pl.BlockSpec((tm, QKV_WIDTH), lambda i: (i, 0)),
        ],
        out_shape=[
            jax.ShapeDtypeStruct((n, REST_WIDTH), F32),
            jax.ShapeDtypeStruct((n, QKV_WIDTH), BF16),
        ],
        compiler_params=_params(
            ("parallel",),
            _nbytes((tm, D_MODEL), F32) + _nbytes((D_MODEL, width), BF16)
            + _nbytes((tm, REST_WIDTH), F32) + _nbytes((tm, QKV_WIDTH), BF16),
            _nbytes((tm, width), F32) + _nbytes((tm, D_MODEL), F32)),
        name="in_proj",
    )(x2, norm_w, w_cat)


def _sb_attn_kernel(q_ref, k_ref, v_ref, upper_ref, o_ref):
    t = upper_ref.shape[0]
    nsub = q_ref.shape[0] // t
    i = pl.program_id(1)
    upper = upper_ref[...]
    row = lax.broadcasted_iota(jnp.int32, (t, t), 0)
    col = lax.broadcasted_iota(jnp.int32, (t, t), 1)
    lane_head = lax.broadcasted_iota(jnp.int32, (1, SB_WIDTH), 1) // SB_HEAD_DIM
    head_mask = [jnp.where(lane_head == h, 1.0, 0.0).astype(BF16) for h in range(SB_HEADS)]

    def block(q_heads, j, accs, out, diagonal):
        start = pl.multiple_of(j * t, t)
        kb = k_ref[pl.ds(start, t), :]
        vb = v_ref[pl.ds(start, t), :]
        new_accs = []
        for h in range(SB_HEADS):
            z = lax.dot_general(q_heads[h], kb, (((1,), (1,)), ((), ())), preferred_element_type=F32)
            log1m = jnp.minimum(-z, 0.0) - jnp.log(1.0 + jnp.exp(-jnp.abs(z)))
            if diagonal:
                mask = col < row
                log1m_m = jnp.where(mask, log1m, 0.0)
            else:
                log1m_m = log1m
            rev = jnp.dot(log1m_m.astype(BF16), upper, preferred_element_type=F32)
            w = jnp.exp(log1m + z + rev + accs[h])
            if diagonal:
                w = jnp.where(mask, w, 0.0)
            out = out + jnp.dot(w.astype(BF16), vb * head_mask[h], preferred_element_type=F32)
            new_accs.append(accs[h] + rev[:, 0:1] + log1m_m[:, 0:1])
        return tuple(new_accs), out

    started = []
    for r in range(nsub):
        qs = q_ref[r * t:(r + 1) * t, :] * (SB_HEAD_DIM ** -0.5)
        q_heads = [qs * head_mask[h] for h in range(SB_HEADS)]
        ib = i * nsub + r
        accs0 = tuple(jnp.zeros((t, 1), F32) for _ in range(SB_HEADS))
        accs0, out0 = block(q_heads, ib, accs0, jnp.zeros((t, SB_WIDTH), F32), True)
        accs1, out1 = block(q_heads, jnp.maximum(ib - 1, 0), accs0, jnp.zeros((t, SB_WIDTH), F32), False)
        has_prev = ib > 0
        accs0 = tuple(jnp.where(has_prev, a1, a0) for a0, a1 in zip(accs0, accs1))
        started.append((q_heads, ib, accs0, out0 + jnp.where(has_prev, out1, 0.0)))

    def cond(carry):
        j, accs, _ = carry
        worst = functools.reduce(jnp.maximum, accs)
        return jnp.logical_and(j >= 0, jnp.max(worst) > SB_SKIP_LOG)

    for r, (q_heads, ib, accs0, out0) in enumerate(started):
        def body(carry, q_heads=q_heads):
            j, accs, out = carry
            accs, out = block(q_heads, j, accs, out, False)
            return j - 1, accs, out

        _, _, out = lax.while_loop(cond, body, (ib - 2, accs0, out0))
        o_ref[r * t:(r + 1) * t, :] = out


def _sb_attention(qkv, batch, upper):
    n = qkv.shape[0]
    s = n // batch
    t = SB_BLOCK
    tq = SB_QUERY_BLOCKS * t
    nq = s // tq
    return pl.pallas_call(
        _sb_attn_kernel,
        grid=(batch, nq),
        in_specs=[
            pl.BlockSpec((tq, SB_WIDTH), lambda b, i: (b * nq + i, 0)),
            pl.BlockSpec((s, SB_WIDTH), lambda b, i: (b, 1)),
            pl.BlockSpec((s, SB_WIDTH), lambda b, i: (b, 2)),
            pl.BlockSpec((t, t), lambda b, i: (0, 0)),
        ],
        out_specs=pl.BlockSpec((tq, SB_WIDTH), lambda b, i: (b * nq + i, 0)),
        out_shape=jax.ShapeDtypeStruct((n, SB_WIDTH), F32),
        compiler_params=_params(
            ("parallel", "arbitrary"),
            2 * _nbytes((s, SB_WIDTH), BF16) + _nbytes((tq, SB_WIDTH), BF16) + _nbytes((t, t), BF16)
            + _nbytes((tq, SB_WIDTH), F32),
            2 * SB_QUERY_BLOCKS * SB_HEADS * _nbytes((t, t), F32)),
        name="sb_attention",
    )(qkv, qkv, qkv, upper)


def _mixer_kernel(rest_ref, ysb_ref, x_ref, convw_ref, convb_ref, dtb_ref, alog_ref, dexp_ref,
                  normw_ref, scw_ref, wout_ref, tri_ref, o_ref,
                  xbc_ext, sc_ext, state_ref):
    t = rest_ref.shape[0]
    sblk = pl.program_id(1)
    halo = SUBLANES

    @pl.when(sblk == 0)
    def _():
        xbc_ext[0:halo, :] = jnp.zeros((halo, SSM_CONV_DIM), F32)
        sc_ext[0:halo, :] = jnp.zeros((halo, SC_WIDTH), F32)
        state_ref[...] = jnp.zeros_like(state_ref)

    o_z = 0
    o_xbc = SSM_WIDTH
    o_dt = o_xbc + SSM_CONV_DIM
    o_sc = o_dt + DT_PAD
    z = rest_ref[:, o_z:o_z + SSM_WIDTH]
    dt_raw = rest_ref[:, o_dt:o_dt + DT_PAD]
    sc_b = rest_ref[:, o_sc:o_sc + SC_WIDTH]
    sc_c = rest_ref[:, o_sc + SC_WIDTH:o_sc + 2 * SC_WIDTH]
    sc_v = rest_ref[:, o_sc + 2 * SC_WIDTH:o_sc + 3 * SC_WIDTH]

    xbc_ext[halo:halo + t, :] = rest_ref[:, o_xbc:o_xbc + SSM_CONV_DIM]
    conv = convb_ref[...]
    for kk in range(SSM_CONV):
        shift = SSM_CONV - 1 - kk
        conv = conv + convw_ref[kk:kk + 1, :] * xbc_ext[halo - shift:halo - shift + t, :]
    tail = xbc_ext[t:t + halo, :]
    xbc_ext[0:halo, :] = tail
    xbc = _silu(conv)
    xs = xbc[:, :SSM_WIDTH]

    sc_ext[halo:halo + t, :] = sc_c * sc_v
    sconv = jnp.zeros((t, SC_WIDTH), F32)
    for kk in range(SC_CONV):
        shift = SC_CONV - 1 - kk
        sconv = sconv + scw_ref[kk:kk + 1, :] * sc_ext[halo - shift:halo - shift + t, :]
    sc_tail = sc_ext[t:t + halo, :]
    sc_ext[0:halo, :] = sc_tail
    y_sc = sc_b * sconv

    dt = _softplus(dt_raw + dtb_ref[...])
    a = dt * (-jnp.exp(alog_ref[...]))
    a_hi = a.astype(BF16)
    a_lo = (a - a_hi.astype(F32)).astype(BF16)
    tri = tri_ref[...]
    a_cs = (jnp.dot(tri, a_hi, preferred_element_type=F32)
            + jnp.dot(tri, a_lo, preferred_element_type=F32))
    a_cs_t = a_cs.T
    a_end = a_cs[t - 1:t, :]
    row = lax.broadcasted_iota(jnp.int32, (t, t), 0)
    col = lax.broadcasted_iota(jnp.int32, (t, t), 1)
    causal = row >= col

    def lanes_per_head(cols, h0, nh):
        return jnp.concatenate(
            [jnp.broadcast_to(cols[:, h:h + 1], (cols.shape[0], SSM_HEAD_DIM)) for h in range(h0, h0 + nh)],
            axis=1)

    hpg = SSM_HEADS // SSM_GROUPS
    gw = hpg * SSM_HEAD_DIM
    lane_head = lax.broadcasted_iota(jnp.int32, (t, gw), 1) // SSM_HEAD_DIM
    y_groups = []
    for g in range(SSM_GROUPS):
        b_g = xbc[:, SSM_WIDTH + g * SSM_STATE:SSM_WIDTH + (g + 1) * SSM_STATE]
        c_g = xbc[:, SSM_WIDTH + SSM_GROUPS * SSM_STATE + g * SSM_STATE:
                  SSM_WIDTH + SSM_GROUPS * SSM_STATE + (g + 1) * SSM_STATE]
        b_bf = b_g.astype(BF16)
        c_bf = c_g.astype(BF16)
        xs_g = xs[:, g * gw:(g + 1) * gw]
        dt_g = lanes_per_head(dt, g * hpg, hpg)
        acs_g = lanes_per_head(a_cs, g * hpg, hpg)
        aend_g = lanes_per_head(a_end, g * hpg, hpg)
        xd_g = xs_g * dt_g
        xd_bf = xd_g.astype(BF16)
        cb = lax.dot_general(c_bf, b_bf, (((1,), (1,)), ((), ())), preferred_element_type=F32)
        y_diag = jnp.zeros((t, gw), F32)
        for hh in range(hpg):
            h = g * hpg + hh
            seg = a_cs[:, h:h + 1] - a_cs_t[h:h + 1, :]
            lmat = jnp.exp(jnp.where(causal, seg, NEG_INF))
            p_h = jnp.dot((cb * lmat).astype(BF16), xd_bf, preferred_element_type=F32)
            y_diag = jnp.where(lane_head == hh, p_h, y_diag)
        state = state_ref[g]
        y_off = jnp.dot(c_bf, state.astype(BF16), preferred_element_type=F32) * jnp.exp(acs_g)
        xdd = (xd_g * jnp.exp(aend_g - acs_g)).astype(BF16)
        contrib = lax.dot_general(b_bf, xdd, (((0,), (0,)), ((), ())), preferred_element_type=F32)
        state_ref[g] = state * jnp.exp(aend_g) + contrib
        y_groups.append(y_diag + y_off + dexp_ref[:, g * gw:(g + 1) * gw] * xs_g)
    y = jnp.concatenate(y_groups, axis=1) * _silu(z)
    y_ssm = _rmsnorm(y, normw_ref[...])

    o0 = SB_WIDTH
    o1 = o0 + SSM_WIDTH
    out = x_ref[...]
    out = out + jnp.dot(ysb_ref[...].astype(BF16), wout_ref[0:o0, :], preferred_element_type=F32)
    out = out + jnp.dot(y_ssm.astype(BF16), wout_ref[o0:o1, :], preferred_element_type=F32)
    out = out + jnp.dot(y_sc.astype(BF16), wout_ref[o1:o1 + SC_WIDTH, :], preferred_element_type=F32)
    o_ref[...] = out


def _mixer(rest, ysb, x2, batch, conv_w, conv_b, dt_bias, a_log, d_exp, norm_w, sc_w, w_out, tri):
    n = x2.shape[0]
    t = MIX_BLOCK
    spb = n // batch // t
    row_map = lambda b, s: (b * spb + s, 0)
    const = lambda b, s: (0, 0)
    return pl.pallas_call(
        _mixer_kernel,
        grid=(batch, spb),
        in_specs=[
            pl.BlockSpec((t, REST_WIDTH), row_map),
            pl.BlockSpec((t, SB_WIDTH), row_map),
            pl.BlockSpec((t, D_MODEL), row_map),
            pl.BlockSpec((SSM_CONV, SSM_CONV_DIM), const),
            pl.BlockSpec((1, SSM_CONV_DIM), const),
            pl.BlockSpec((1, DT_PAD), const),
            pl.BlockSpec((1, DT_PAD), const),
            pl.BlockSpec((1, SSM_WIDTH), const),
            pl.BlockSpec((1, SSM_WIDTH), const),
            pl.BlockSpec((SC_CONV, SC_WIDTH), const),
            pl.BlockSpec((D_MODEL, D_MODEL), const),
            pl.BlockSpec((t, t), const),
        ],
        out_specs=pl.BlockSpec((t, D_MODEL), row_map),
        out_shape=jax.ShapeDtypeStruct((n, D_MODEL), F32),
        scratch_shapes=[
            pltpu.VMEM((t + SUBLANES, SSM_CONV_DIM), F32),
            pltpu.VMEM((t + SUBLANES, SC_WIDTH), F32),
            pltpu.VMEM((SSM_GROUPS, SSM_STATE, (SSM_HEADS // SSM_GROUPS) * SSM_HEAD_DIM), F32),
        ],
        compiler_params=_params(
            ("arbitrary", "arbitrary"),
            _nbytes((t, REST_WIDTH), F32) + _nbytes((t, SB_WIDTH), F32) + 2 * _nbytes((t, D_MODEL), F32)
            + _nbytes((D_MODEL, D_MODEL), BF16) + _nbytes((t, t), BF16),
            _nbytes((t + SUBLANES, SSM_CONV_DIM + SC_WIDTH), F32) + 4 * _nbytes((t, t), F32)
            + _nbytes((t, REST_WIDTH), F32)),
        name="mixer",
    )(rest, ysb, x2, conv_w, conv_b, dt_bias, a_log, d_exp, norm_w, sc_w, w_out, tri)


def _sort16_pairs():
    def merge(lo, hi, r):
        step = r * 2
        if step < hi - lo:
            yield from merge(lo, hi, step)
            yield from merge(lo + r, hi, step)
            for i in range(lo + r, hi - r, step):
                yield (i, i + r)
        else:
            yield (lo, lo + r)

    def sort(lo, hi):
        if hi - lo >= 1:
            mid = lo + (hi - lo) // 2
            yield from sort(lo, mid)
            yield from sort(mid + 1, hi)
            yield from merge(lo, hi, 1)

    return tuple(sort(0, PEER_TOPK - 1))


_SORT16 = _sort16_pairs()


def _exchange(v, i, j):
    v[i], v[j] = jnp.maximum(v[i], v[j]), jnp.minimum(v[i], v[j])


def _merge_sublane_lists(v):
    for shift in (4, 2, 1):
        r = [pltpu.roll(x, shift, axis=0) for x in v]
        v = [jnp.maximum(v[i], r[PEER_TOPK - 1 - i]) for i in range(PEER_TOPK)]
        d = PEER_TOPK // 2
        while d >= 1:
            for i in range(PEER_TOPK):
                if not i & d:
                    _exchange(v, i, i + d)
            d //= 2
    return v


def _top16_keys(scores):
    v = [scores[k * SUBLANES:(k + 1) * SUBLANES, :] for k in range(PEER_NKEYS // SUBLANES)]
    for i, j in _SORT16:
        _exchange(v, i, j)
    return _merge_sublane_lists(v)


def _top16_pairs(top0, top1):
    t = top0[0].shape[1]
    sub = lax.broadcasted_iota(jnp.int32, (SUBLANES, t), 0)
    lengths = (16, 8, 5, 4, 12, 4, 1, 0)
    length = jnp.zeros((SUBLANES, t), jnp.int32)
    for s_, n_ in enumerate(lengths):
        length = jnp.where(sub == s_, n_, length)
    row_part = jnp.where(sub == 0, top0[0], jnp.where(sub == 1, top0[1],
                                                     jnp.where(sub == 2, top0[2], top0[3])))
    col_part = jnp.where(sub == 4, top1[0], jnp.where(sub == 5, top1[1], top1[2]))
    lists = []
    for r in range(PEER_TOPK):
        a = jnp.where(sub < 4, row_part, top0[min(4 + r, PEER_TOPK - 1)])
        b = jnp.where(sub < 4, top1[r], col_part)
        lists.append(jnp.where(r < length, a + b, NEG_INF))
    return _merge_sublane_lists(lists)


def _split_bf16(x):
    hi = x.astype(BF16)
    return hi, (x - hi.astype(F32)).astype(BF16)


def _dot3(a_hi, a_lo, b_hi, b_lo):
    return (jnp.dot(a_hi, b_hi, preferred_element_type=F32)
            + jnp.dot(a_hi, b_lo, preferred_element_type=F32)
            + jnp.dot(a_lo, b_hi, preferred_element_type=F32))


def _fold_keys_kernel(keys_ref, wqt_ref, hi_ref, lo_ref):
    k_hi, k_lo = _split_bf16(keys_ref[...])
    w_hi, w_lo = _split_bf16(wqt_ref[...])
    hi_ref[...], lo_ref[...] = _split_bf16(_dot3(k_hi, k_lo, w_hi, w_lo))


def _fold_keys(wq_t, keys):
    half = PEER_KEY_DIM // 2
    nblk = wq_t.shape[0] // half
    out = jax.ShapeDtypeStruct((nblk * PEER_NKEYS, D_MODEL), BF16)
    out_spec = pl.BlockSpec((PEER_NKEYS, D_MODEL), lambda i: (i, 0))
    return pl.pallas_call(
        _fold_keys_kernel,
        grid=(nblk,),
        in_specs=[
            pl.BlockSpec((None, PEER_NKEYS, half), lambda i: (i % 2, 0, 0)),
            pl.BlockSpec((half, D_MODEL), lambda i: (i, 0)),
        ],
        out_specs=[out_spec, out_spec],
        out_shape=[out, out],
        compiler_params=_params(
            ("parallel",),
            _nbytes((PEER_NKEYS, half), F32) + _nbytes((half, D_MODEL), F32)
            + 2 * _nbytes((PEER_NKEYS, D_MODEL), BF16),
            2 * _nbytes((PEER_NKEYS, D_MODEL), F32)),
        name="peer_fold_keys",
    )(keys, wq_t)


def _route_kernel(x_ref, nw_ref, wk_hi_ref, wk_lo_ref, ht_ref, n0_ref, w0_ref, r1_ref, e1_ref):
    h = _rmsnorm(x_ref[...], nw_ref[...])
    ht_hi, ht_lo = _split_bf16(h.T)
    ht_ref[...] = ht_hi
    scores = _dot3(wk_hi_ref[...], wk_lo_ref[...], ht_hi, ht_lo)
    for hh in range(PEER_HEADS):
        s0 = scores[(2 * hh) * PEER_NKEYS:(2 * hh + 1) * PEER_NKEYS, :]
        s1 = scores[(2 * hh + 1) * PEER_NKEYS:(2 * hh + 2) * PEER_NKEYS, :]
        for l0 in range(0, scores.shape[1], LANES):
            ln = slice(l0, l0 + LANES)
            _route_lane_tile(s0[:, ln], s1[:, ln], hh, ln, n0_ref, w0_ref, r1_ref, e1_ref)


def _count_leading(t, pred):
    c8 = pred(t[7])
    c4 = pred(jnp.where(c8, t[11], t[3]))
    c2 = pred(jnp.where(c8, jnp.where(c4, t[13], t[9]), jnp.where(c4, t[5], t[1])))
    c1 = pred(jnp.where(c8,
                        jnp.where(c4, jnp.where(c2, t[14], t[12]), jnp.where(c2, t[10], t[8])),
                        jnp.where(c4, jnp.where(c2, t[6], t[4]), jnp.where(c2, t[2], t[0]))))
    count = (jnp.where(c8, 8.0, 0.0) + jnp.where(c4, 4.0, 0.0)
             + jnp.where(c2, 2.0, 0.0) + jnp.where(c1, 1.0, 0.0))
    return count + jnp.where(pred(t[15]), 1.0, 0.0)


def _route_lane_tile(s0, s1, hh, ln, n0_ref, w0_ref, r1_ref, e1_ref):
    top0 = _top16_keys(s0)
    top1 = _top16_keys(s1)
    best = _top16_pairs(top0, top1)
    z = jnp.exp(best[0] - best[0])
    for r in range(1, PEER_TOPK):
        z = z + jnp.exp(best[r] - best[0])
    tau = best[PEER_TOPK - 1]
    scale0 = INV_SQRT2 / z
    r1_rows, e1_rows = [], []
    for k in range(PEER_NKEYS // SUBLANES):
        rows = slice(k * SUBLANES, (k + 1) * SUBLANES)
        s0k = s0[rows, :]
        s1k = s1[rows, :]
        n0_ref[hh, rows, ln] = _count_leading(top1, lambda m: s0k + m >= tau)
        w0_ref[hh, rows, ln] = jnp.exp(s0k - top0[0]) * scale0
        r1_rows.append(_count_leading(top1, lambda m: m > s1k))
        e1_rows.append(jnp.exp(s1k - top1[0]))
    r1_ref[hh, :, ln] = jnp.concatenate(r1_rows, axis=0).astype(BF16)
    e1_ref[hh, :, ln] = jnp.concatenate(e1_rows, axis=0).astype(BF16)


def _route(x2, norm_w, wq_t, keys):
    n = x2.shape[0]
    t = ROUTE_BLOCK
    hk = (PEER_HEADS, PEER_NKEYS, n)
    hk_spec = pl.BlockSpec((PEER_HEADS, PEER_NKEYS, t), lambda i: (0, 0, i))
    wk_hi, wk_lo = _fold_keys(wq_t, keys)
    wk_spec = pl.BlockSpec(wk_hi.shape, lambda i: (0, 0))
    return pl.pallas_call(
        _route_kernel,
        grid=(n // t,),
        in_specs=[
            pl.BlockSpec((t, D_MODEL), lambda i: (i, 0)),
            pl.BlockSpec((1, D_MODEL), lambda i: (0, 0)),
            wk_spec, wk_spec,
        ],
        out_specs=[
            pl.BlockSpec((D_MODEL, t), lambda i: (0, i)),
            hk_spec, hk_spec, hk_spec, hk_spec,
        ],
        out_shape=[
            jax.ShapeDtypeStruct((D_MODEL, n), BF16),
            jax.ShapeDtypeStruct(hk, F32), jax.ShapeDtypeStruct(hk, F32),
            jax.ShapeDtypeStruct(hk, BF16), jax.ShapeDtypeStruct(hk, BF16),
        ],
        compiler_params=_params(
            ("parallel",),
            _nbytes((t, D_MODEL), F32) + 2 * _nbytes(wk_hi.shape, BF16) + _nbytes((D_MODEL, t), BF16)
            + 2 * _nbytes((PEER_HEADS, PEER_NKEYS, t), F32) + 2 * _nbytes((PEER_HEADS, PEER_NKEYS, t), BF16),
            3 * _nbytes((D_MODEL, t), F32)),
        name="peer_route",
    )(x2, norm_w, wk_hi, wk_lo)


def _bf16_rows(row, n):
    packed = 2 * SUBLANES
    tile = jnp.broadcast_to(row, (packed, row.shape[1])).astype(BF16)
    return jnp.concatenate([tile] * (n // packed), axis=0)


def _expert_kernel(ht_ref, n0_ref, w0_ref, r1_ref, e1_ref, u_ref, vt_ref, x_ref, fw_ref,
                   o_ref, acc_ref, *, final_norm):
    e = pl.program_id(1)
    eb = u_ref.shape[0]
    t = ht_ref.shape[1]

    @pl.when(e == 0)
    def _():
        acc_ref[...] = jnp.zeros_like(acc_ref)

    def gated(y, sub):
        gate = jnp.zeros((PEER_NKEYS, t), BF16)
        for hh in range(PEER_HEADS):
            n0 = _bf16_rows(n0_ref[hh, sub:sub + 1, :], PEER_NKEYS)
            w0 = _bf16_rows(w0_ref[hh, sub:sub + 1, :], PEER_NKEYS)
            gate = gate + jnp.where(r1_ref[hh] < n0, e1_ref[hh], 0.0) * w0
        return gate * (y * (1.0 + lax.erf(y))).astype(BF16)

    half = eb // 2
    nsub = half // PEER_NKEYS
    ht = ht_ref[...]
    y_a = jnp.dot(u_ref[0:half, :], ht, preferred_element_type=F32)
    rows_a = [gated(y_a[s * PEER_NKEYS:(s + 1) * PEER_NKEYS, :], s) for s in range(nsub)]
    bits = pltpu.bitcast(rows_a[0][0:2 * SUBLANES, 0:LANES], jnp.uint32)
    sixteen = jnp.uint32(16)
    zero = pltpu.bitcast(lax.shift_right_logical(lax.shift_right_logical(bits, sixteen), sixteen), BF16)
    ht_b = ht + jnp.tile(zero, (ht.shape[0] // zero.shape[0], t // LANES))
    y_b = jnp.dot(u_ref[half:eb, :], ht_b, preferred_element_type=F32)
    rows_b = [gated(y_b[s * PEER_NKEYS:(s + 1) * PEER_NKEYS, :], nsub + s) for s in range(nsub)]
    weighted = jnp.concatenate(rows_a + rows_b, axis=0)
    acc_ref[...] += jnp.dot(vt_ref[...], weighted, preferred_element_type=F32)

    @pl.when(e == pl.num_programs(1) - 1)
    def _():
        out = x_ref[...] + acc_ref[...].T
        if final_norm:
            out = _rmsnorm(out, fw_ref[...])
        o_ref[...] = out


def _experts(ht, n0, w0, r1, e1, u_bf, vt_bf, x2, final_w, final_norm):
    n = x2.shape[0]
    t = PEER_TOKENS
    eb = PEER_EXPERTS_PER_STEP
    ne = u_bf.shape[0] // eb
    hk_spec = pl.BlockSpec((PEER_HEADS, PEER_NKEYS, t), lambda i, e: (0, 0, i))
    ha_spec = pl.BlockSpec((PEER_HEADS, eb // PEER_NKEYS, t), lambda i, e: (0, e, i))
    return pl.pallas_call(
        functools.partial(_expert_kernel, final_norm=final_norm),
        grid=(n // t, ne),
        in_specs=[
            pl.BlockSpec((D_MODEL, t), lambda i, e: (0, i)),
            ha_spec, ha_spec, hk_spec, hk_spec,
            pl.BlockSpec((eb, D_MODEL), lambda i, e: (e, 0)),
            pl.BlockSpec((D_MODEL, eb), lambda i, e: (0, e)),
            pl.BlockSpec((t, D_MODEL), lambda i, e: (i, 0)),
            pl.BlockSpec((1, D_MODEL), lambda i, e: (0, 0)),
        ],
        out_specs=pl.BlockSpec((t, D_MODEL), lambda i, e: (i, 0)),
        out_shape=jax.ShapeDtypeStruct((n, D_MODEL), F32),
        scratch_shapes=[pltpu.VMEM((D_MODEL, t), F32)],
        compiler_params=_params(
            ("parallel", "arbitrary"),
            _nbytes((D_MODEL, t), BF16) + 2 * _nbytes((PEER_HEADS, eb // PEER_NKEYS, t), F32)
            + 2 * _nbytes((PEER_HEADS, PEER_NKEYS, t), BF16) + 2 * _nbytes((eb, D_MODEL), BF16)
            + 2 * _nbytes((t, D_MODEL), F32),
            _nbytes((D_MODEL, t), F32) + _nbytes((eb, t), F32) + _nbytes((eb, t), BF16)),
        name="peer_experts",
    )(ht, n0, w0, r1, e1, u_bf, vt_bf, x2, final_w)


def _pad_lanes(v, width):
    return jnp.pad(v, ((0, 0), (0, width - v.shape[1])))


def kernel(x, norm1_w, w_in, ssm_conv_w, ssm_conv_b, ssm_dt_bias, ssm_A_log, ssm_D, ssm_norm_w,
           sc_conv_w, w_out, norm2_w, peer_w_q, peer_sub_keys, peer_u, peer_v, final_norm_w):
    batch, seq, d = x.shape
    depth = w_in.shape[0]
    n = batch * seq
    x2 = x.reshape(n, d)

    cuts = [0]
    for width in (SB_WIDTH, SB_WIDTH, SB_WIDTH, SSM_WIDTH, SSM_CONV_DIM, SSM_HEADS,
                  SC_WIDTH, SC_WIDTH, SC_WIDTH):
        cuts.append(cuts[-1] + width)
    ident = jnp.arange(SB_BLOCK)
    upper = (ident[:, None] > ident[None, :]).astype(BF16)
    identm = jnp.arange(MIX_BLOCK)
    tri = (identm[None, :] <= identm[:, None]).astype(BF16)

    for layer in range(depth):
        w = w_in[layer]
        w_cat = jnp.concatenate([
            w[:, cuts[3]:cuts[5]],
            _pad_lanes(w[:, cuts[5]:cuts[6]], DT_PAD),
            w[:, cuts[6]:cuts[9]],
            w[:, cuts[0]:cuts[3]],
        ], axis=1).astype(BF16)
        rest, qkv = _in_proj(x2, norm1_w[layer][None, :], w_cat)

        ysb = _sb_attention(qkv, batch, upper)

        x2 = _mixer(
            rest, ysb, x2, batch,
            ssm_conv_w[layer], ssm_conv_b[layer][None, :],
            _pad_lanes(ssm_dt_bias[layer][None, :], DT_PAD),
            _pad_lanes(ssm_A_log[layer][None, :], DT_PAD),
            jnp.repeat(ssm_D[layer], SSM_HEAD_DIM)[None, :],
            ssm_norm_w[layer][None, :], sc_conv_w[layer],
            w_out[layer].astype(BF16), tri)

        ht, n0, w0, r1, e1 = _route(
            x2, norm2_w[layer][None, :], peer_w_q[layer].T, peer_sub_keys[layer])
        x2 = _experts(ht, n0, w0, r1, e1,
                      (peer_u[layer] * INV_SQRT2).astype(BF16), peer_v[layer].T.astype(BF16),
                      x2, final_norm_w[None, :], layer == depth - 1)
    return x2.reshape(batch, seq, d)
```

```python
import functools
import math

import jax
import jax.numpy as jnp
from jax import lax
from jax.experimental import pallas as pl
from jax.experimental.pallas import tpu as pltpu

F32 = jnp.float32
BF16 = jnp.bfloat16

EPS = 1e-6
D_MODEL = 1024
SB_HEADS = 4
SB_HEAD_DIM = 64
SB_WIDTH = SB_HEADS * SB_HEAD_DIM
SSM_HEADS = 8
SSM_HEAD_DIM = 64
SSM_WIDTH = SSM_HEADS * SSM_HEAD_DIM
SSM_GROUPS = 2
SSM_STATE = 64
SSM_CONV = 4
SSM_CONV_DIM = SSM_WIDTH + 2 * SSM_GROUPS * SSM_STATE
SC_WIDTH = 256
SC_CONV = 3
PEER_HEADS = 4
PEER_KEY_DIM = 256
PEER_NKEYS = 128
PEER_TOPK = 16

LANES = 128
SUBLANES = 8
DT_PAD = LANES
REST_WIDTH = SSM_WIDTH + SSM_CONV_DIM + DT_PAD + 3 * SC_WIDTH
QKV_WIDTH = 3 * SB_WIDTH
VMEM_LIMIT_MAX = 56 * 1024 * 1024
VMEM_SPILL_HEADROOM = 8 * 1024 * 1024

IN_PROJ_ROWS = 512
SB_BLOCK = 256
SB_QUERY_BLOCKS = 4
MIX_BLOCK = 256
PEER_TOKENS = 512
PEER_EXPERTS_PER_STEP = 2048
SB_SKIP_LOG = -120.0
INV_SQRT2 = 1.0 / math.sqrt(2.0)
NEG_INF = float("-inf")


def _rmsnorm(x, w):
    return x * lax.rsqrt(jnp.mean(x * x, axis=-1, keepdims=True) + EPS) * w


def _softplus(x):
    return jnp.maximum(x, 0.0) + jnp.log(1.0 + jnp.exp(-jnp.abs(x)))


def _silu(x):
    return x * (1.0 / (1.0 + jnp.exp(-x)))


def _nbytes(shape, dtype):
    return math.prod(shape) * jnp.dtype(dtype).itemsize


def _params(semantics, pipelined_bytes, resident_bytes):
    limit = min(2 * pipelined_bytes + resident_bytes + VMEM_SPILL_HEADROOM, VMEM_LIMIT_MAX)
    return pltpu.CompilerParams(dimension_semantics=semantics, vmem_limit_bytes=limit)


def _in_proj_kernel(x_ref, nw_ref, w_ref, rest_ref, qkv_ref):
    h = _rmsnorm(x_ref[...], nw_ref[...]).astype(BF16)
    p = jnp.dot(h, w_ref[...], preferred_element_type=F32)
    rest_ref[...] = p[:, :REST_WIDTH]
    qkv_ref[...] = p[:, REST_WIDTH:].astype(BF16)


def _in_proj(x2, norm_w, w_cat):
    n = x2.shape[0]
    tm = IN_PROJ_ROWS
    width = w_cat.shape[1]
    return pl.pallas_call(
        _in_proj_kernel,
        grid=(n // tm,),
        in_specs=[
            pl.BlockSpec((tm, D_MODEL), lambda i: (i, 0)),
            pl.BlockSpec((1, D_MODEL), lambda i: (0, 0)),
            pl.BlockSpec((D_MODEL, width), lambda i: (0, 0)),
        ],
        out_specs=[
            pl.BlockSpec((tm, REST_WIDTH), lambda i: (i, 0)),
            pl.BlockSpec((tm, QKV_WIDTH), lambda i: (i, 0)),
        ],
        out_shape=[
            jax.ShapeDtypeStruct((n, REST_WIDTH), F32),
            jax.ShapeDtypeStruct((n, QKV_WIDTH), BF16),
        ],
        compiler_params=_params(
            ("parallel",),
            _nbytes((tm, D_MODEL), F32) + _nbytes((D_MODEL, width), BF16)
            + _nbytes((tm, REST_WIDTH), F32) + _nbytes((tm, QKV_WIDTH), BF16),
            _nbytes((tm, width), F32) + _nbytes((tm, D_MODEL), F32)),
        name="in_proj",
    )(x2, norm_w, w_cat)


def _sb_attn_kernel(q_ref, k_ref, v_ref, upper_ref, o_ref):
    t = upper_ref.shape[0]
    nsub = q_ref.shape[0] // t
    i = pl.program_id(1)
    upper = upper_ref[...]
    row = lax.broadcasted_iota(jnp.int32, (t, t), 0)
    col = lax.broadcasted_iota(jnp.int32, (t, t), 1)
    lane_head = lax.broadcasted_iota(jnp.int32, (1, SB_WIDTH), 1) // SB_HEAD_DIM
    head_mask = [jnp.where(lane_head == h, 1.0, 0.0).astype(BF16) for h in range(SB_HEADS)]

    def block(q_heads, j, accs, out, diagonal):
        start = pl.multiple_of(j * t, t)
        kb = k_ref[pl.ds(start, t), :]
        vb = v_ref[pl.ds(start, t), :]
        new_accs = []
        for h in range(SB_HEADS):
            z = lax.dot_general(q_heads[h], kb, (((1,), (1,)), ((), ())), preferred_element_type=F32)
            log1m = jnp.minimum(-z, 0.0) - jnp.log(1.0 + jnp.exp(-jnp.abs(z)))
            if diagonal:
                mask = col < row
                log1m_m = jnp.where(mask, log1m, 0.0)
            else:
                log1m_m = log1m
            rev = jnp.dot(log1m_m.astype(BF16), upper, preferred_element_type=F32)
            w = jnp.exp(log1m + z + rev + accs[h])
            if diagonal:
                w = jnp.where(mask, w, 0.0)
            out = out + jnp.dot(w.astype(BF16), vb * head_mask[h], preferred_element_type=F32)
            new_accs.append(accs[h] + rev[:, 0:1] + log1m_m[:, 0:1])
        return tuple(new_accs), out

    started = []
    for r in range(nsub):
        qs = q_ref[r * t:(r + 1) * t, :] * (SB_HEAD_DIM ** -0.5)
        q_heads = [qs * head_mask[h] for h in range(SB_HEADS)]
        ib = i * nsub + r
        accs0 = tuple(jnp.zeros((t, 1), F32) for _ in range(SB_HEADS))
        accs0, out0 = block(q_heads, ib, accs0, jnp.zeros((t, SB_WIDTH), F32), True)
        accs1, out1 = block(q_heads, jnp.maximum(ib - 1, 0), accs0, jnp.zeros((t, SB_WIDTH), F32), False)
        has_prev = ib > 0
        accs0 = tuple(jnp.where(has_prev, a1, a0) for a0, a1 in zip(accs0, accs1))
        started.append((q_heads, ib, accs0, out0 + jnp.where(has_prev, out1, 0.0)))

    def cond(carry):
        j, accs, _ = carry
        worst = functools.reduce(jnp.maximum, accs)
        return jnp.logical_and(j >= 0, jnp.max(worst) > SB_SKIP_LOG)

    for r, (q_heads, ib, accs0, out0) in enumerate(started):
        def body(carry, q_heads=q_heads):
            j, accs, out = carry
            accs, out = block(q_heads, j, accs, out, False)
            return j - 1, accs, out

        _, _, out = lax.while_loop(cond, body, (ib - 2, accs0, out0))
        o_ref[r * t:(r + 1) * t, :] = out


def _sb_attention(qkv, batch, upper):
    n = qkv.shape[0]
    s = n // batch
    t = SB_BLOCK
    tq = SB_QUERY_BLOCKS * t
    nq = s // tq
    return pl.pallas_call(
        _sb_attn_kernel,
        grid=(batch, nq),
        in_specs=[
            pl.BlockSpec((tq, SB_WIDTH), lambda b, i: (b * nq + i, 0)),
            pl.BlockSpec((s, SB_WIDTH), lambda b, i: (b, 1)),
            pl.BlockSpec((s, SB_WIDTH), lambda b, i: (b, 2)),
            pl.BlockSpec((t, t), lambda b, i: (0, 0)),
        ],
        out_specs=pl.BlockSpec((tq, SB_WIDTH), lambda b, i: (b * nq + i, 0)),
        out_shape=jax.ShapeDtypeStruct((n, SB_WIDTH), F32),
        compiler_params=_params(
            ("parallel", "arbitrary"),
            2 * _nbytes((s, SB_WIDTH), BF16) + _nbytes((tq, SB_WIDTH), BF16) + _nbytes((t, t), BF16)
            + _nbytes((tq, SB_WIDTH), F32),
            2 * SB_QUERY_BLOCKS * SB_HEADS * _nbytes((t, t), F32)),
        name="sb_attention",
    )(qkv, qkv, qkv, upper)


def _mixer_route_kernel(rest_ref, ysb_ref, x_ref, convw_ref, convb_ref, dtb_ref, alog_ref, dexp_ref,
                        normw_ref, scw_ref, wout_ref, tri_ref, nw2_ref, wk_hi_ref, wk_lo_ref,
                        o_ref, ht_ref, n0_ref, w0_ref, r1_ref, e1_ref,
                        xbc_ext, sc_ext, state_ref, xprev_ref, *, steps_per_row):
    s = pl.program_id(0)
    halo = SUBLANES

    @pl.when(s % steps_per_row == 0)
    def _():
        xbc_ext[0:halo, :] = jnp.zeros((halo, SSM_CONV_DIM), F32)
        sc_ext[0:halo, :] = jnp.zeros((halo, SC_WIDTH), F32)
        state_ref[...] = jnp.zeros_like(state_ref)

    @pl.when(s == 0)
    def _():
        xprev_ref[...] = jnp.zeros_like(xprev_ref)

    _route_kernel(xprev_ref, nw2_ref, wk_hi_ref, wk_lo_ref, ht_ref, n0_ref, w0_ref, r1_ref, e1_ref)
    _mixer_body(rest_ref, ysb_ref, x_ref, convw_ref, convb_ref, dtb_ref, alog_ref, dexp_ref,
                normw_ref, scw_ref, wout_ref, tri_ref, o_ref, xbc_ext, sc_ext, state_ref)
    xprev_ref[...] = o_ref[...]


def _mixer_body(rest_ref, ysb_ref, x_ref, convw_ref, convb_ref, dtb_ref, alog_ref, dexp_ref,
                normw_ref, scw_ref, wout_ref, tri_ref, o_ref,
                xbc_ext, sc_ext, state_ref):
    t = rest_ref.shape[0]
    halo = SUBLANES

    o_z = 0
    o_xbc = SSM_WIDTH
    o_dt = o_xbc + SSM_CONV_DIM
    o_sc = o_dt + DT_PAD
    z = rest_ref[:, o_z:o_z + SSM_WIDTH]
    dt_raw = rest_ref[:, o_dt:o_dt + DT_PAD]
    sc_b = rest_ref[:, o_sc:o_sc + SC_WIDTH]
    sc_c = rest_ref[:, o_sc + SC_WIDTH:o_sc + 2 * SC_WIDTH]
    sc_v = rest_ref[:, o_sc + 2 * SC_WIDTH:o_sc + 3 * SC_WIDTH]

    xbc_ext[halo:halo + t, :] = rest_ref[:, o_xbc:o_xbc + SSM_CONV_DIM]
    conv = convb_ref[...]
    for kk in range(SSM_CONV):
        shift = SSM_CONV - 1 - kk
        conv = conv + convw_ref[kk:kk + 1, :] * xbc_ext[halo - shift:halo - shift + t, :]
    tail = xbc_ext[t:t + halo, :]
    xbc_ext[0:halo, :] = tail
    xbc = _silu(conv)
    xs = xbc[:, :SSM_WIDTH]

    sc_ext[halo:halo + t, :] = sc_c * sc_v
    sconv = jnp.zeros((t, SC_WIDTH), F32)
    for kk in range(SC_CONV):
        shift = SC_CONV - 1 - kk
        sconv = sconv + scw_ref[kk:kk + 1, :] * sc_ext[halo - shift:halo - shift + t, :]
    sc_tail = sc_ext[t:t + halo, :]
    sc_ext[0:halo, :] = sc_tail
    y_sc = sc_b * sconv

    dt = _softplus(dt_raw + dtb_ref[...])
    a = dt * (-jnp.exp(alog_ref[...]))
    a_hi = a.astype(BF16)
    a_lo = (a - a_hi.astype(F32)).astype(BF16)
    tri = tri_ref[...]
    a_cs = (jnp.dot(tri, a_hi, preferred_element_type=F32)
            + jnp.dot(tri, a_lo, preferred_element_type=F32))
    a_cs_t = a_cs.T
    a_end = a_cs[t - 1:t, :]
    row = lax.broadcasted_iota(jnp.int32, (t, t), 0)
    col = lax.broadcasted_iota(jnp.int32, (t, t), 1)
    causal = row >= col

    def lanes_per_head(cols, h0, nh):
        return jnp.concatenate(
            [jnp.broadcast_to(cols[:, h:h + 1], (cols.shape[0], SSM_HEAD_DIM)) for h in range(h0, h0 + nh)],
            axis=1)

    hpg = SSM_HEADS // SSM_GROUPS
    gw = hpg * SSM_HEAD_DIM
    lane_head = lax.broadcasted_iota(jnp.int32, (t, gw), 1) // SSM_HEAD_DIM
    y_groups = []
    for g in range(SSM_GROUPS):
        b_g = xbc[:, SSM_WIDTH + g * SSM_STATE:SSM_WIDTH + (g + 1) * SSM_STATE]
        c_g = xbc[:, SSM_WIDTH + SSM_GROUPS * SSM_STATE + g * SSM_STATE:
                  SSM_WIDTH + SSM_GROUPS * SSM_STATE + (g + 1) * SSM_STATE]
        b_bf = b_g.astype(BF16)
        c_bf = c_g.astype(BF16)
        xs_g = xs[:, g * gw:(g + 1) * gw]
        dt_g = lanes_per_head(dt, g * hpg, hpg)
        acs_g = lanes_per_head(a_cs, g * hpg, hpg)
        aend_g = lanes_per_head(a_end, g * hpg, hpg)
        xd_g = xs_g * dt_g
        xd_bf = xd_g.astype(BF16)
        cb = lax.dot_general(c_bf, b_bf, (((1,), (1,)), ((), ())), preferred_element_type=F32)
        y_diag = jnp.zeros((t, gw), F32)
        for hh in range(hpg):
            h = g * hpg + hh
            seg = a_cs[:, h:h + 1] - a_cs_t[h:h + 1, :]
            lmat = jnp.exp(jnp.where(causal, seg, NEG_INF))
            p_h = jnp.dot((cb * lmat).astype(BF16), xd_bf, preferred_element_type=F32)
            y_diag = jnp.where(lane_head == hh, p_h, y_diag)
        state = state_ref[g]
        y_off = jnp.dot(c_bf, state.astype(BF16), preferred_element_type=F32) * jnp.exp(acs_g)
        xdd = (xd_g * jnp.exp(aend_g - acs_g)).astype(BF16)
        contrib = lax.dot_general(b_bf, xdd, (((0,), (0,)), ((), ())), preferred_element_type=F32)
        state_ref[g] = state * jnp.exp(aend_g) + contrib
        y_groups.append(y_diag + y_off + dexp_ref[:, g * gw:(g + 1) * gw] * xs_g)
    y = jnp.concatenate(y_groups, axis=1) * _silu(z)
    y_ssm = _rmsnorm(y, normw_ref[...])

    o0 = SB_WIDTH
    o1 = o0 + SSM_WIDTH
    out = x_ref[...]
    out = out + jnp.dot(ysb_ref[...].astype(BF16), wout_ref[0:o0, :], preferred_element_type=F32)
    out = out + jnp.dot(y_ssm.astype(BF16), wout_ref[o0:o1, :], preferred_element_type=F32)
    out = out + jnp.dot(y_sc.astype(BF16), wout_ref[o1:o1 + SC_WIDTH, :], preferred_element_type=F32)
    o_ref[...] = out


def _mixer_route(rest, ysb, x2, batch, conv_w, conv_b, dt_bias, a_log, d_exp, norm_w, sc_w, w_out, tri,
                 norm2_w, wq_t, keys):
    n = x2.shape[0]
    t = MIX_BLOCK
    steps = n // t
    spb = steps // batch
    wk_hi, wk_lo = _fold_keys(wq_t, keys)
    mix_map = lambda s: (jnp.minimum(s, steps - 1), 0)
    route_map = lambda s: (0, 0, jnp.maximum(s - 1, 0))
    const = lambda s: (0, 0)
    hk = (PEER_HEADS, PEER_NKEYS, n)
    hk_spec = pl.BlockSpec((PEER_HEADS, PEER_NKEYS, t), route_map)
    return pl.pallas_call(
        functools.partial(_mixer_route_kernel, steps_per_row=spb),
        grid=(steps + 1,),
        in_specs=[
            pl.BlockSpec((t, REST_WIDTH), mix_map),
            pl.BlockSpec((t, SB_WIDTH), mix_map),
            pl.BlockSpec((t, D_MODEL), mix_map),
            pl.BlockSpec((SSM_CONV, SSM_CONV_DIM), const),
            pl.BlockSpec((1, SSM_CONV_DIM), const),
            pl.BlockSpec((1, DT_PAD), const),
            pl.BlockSpec((1, DT_PAD), const),
            pl.BlockSpec((1, SSM_WIDTH), const),
            pl.BlockSpec((1, SSM_WIDTH), const),
            pl.BlockSpec((SC_CONV, SC_WIDTH), const),
            pl.BlockSpec((D_MODEL, D_MODEL), const),
            pl.BlockSpec((t, t), const),
            pl.BlockSpec((1, D_MODEL), const),
            pl.BlockSpec(wk_hi.shape, const),
            pl.BlockSpec(wk_lo.shape, const),
        ],
        out_specs=[
            pl.BlockSpec((t, D_MODEL), lambda s: (s, 0)),
            pl.BlockSpec((D_MODEL, t), lambda s: (0, jnp.maximum(s - 1, 0))),
            hk_spec, hk_spec, hk_spec, hk_spec,
        ],
        out_shape=[
            jax.ShapeDtypeStruct((n + t, D_MODEL), F32),
            jax.ShapeDtypeStruct((D_MODEL, n), BF16),
            jax.ShapeDtypeStruct(hk, F32), jax.ShapeDtypeStruct(hk, F32),
            jax.ShapeDtypeStruct(hk, BF16), jax.ShapeDtypeStruct(hk, BF16),
        ],
        scratch_shapes=[
            pltpu.VMEM((t + SUBLANES, SSM_CONV_DIM), F32),
            pltpu.VMEM((t + SUBLANES, SC_WIDTH), F32),
            pltpu.VMEM((SSM_GROUPS, SSM_STATE, (SSM_HEADS // SSM_GROUPS) * SSM_HEAD_DIM), F32),
            pltpu.VMEM((t, D_MODEL), F32),
        ],
        compiler_params=_params(
            ("arbitrary",),
            _nbytes((t, REST_WIDTH), F32) + _nbytes((t, SB_WIDTH), F32) + 2 * _nbytes((t, D_MODEL), F32)
            + _nbytes((D_MODEL, D_MODEL), BF16) + _nbytes((t, t), BF16)
            + 2 * _nbytes(wk_hi.shape, BF16) + _nbytes((D_MODEL, t), BF16)
            + 2 * _nbytes((PEER_HEADS, PEER_NKEYS, t), F32) + 2 * _nbytes((PEER_HEADS, PEER_NKEYS, t), BF16),
            _nbytes((t + SUBLANES, SSM_CONV_DIM + SC_WIDTH), F32) + 4 * _nbytes((t, t), F32)
            + _nbytes((t, REST_WIDTH), F32) + 4 * _nbytes((D_MODEL, t), F32)),
        name="mixer_route",
    )(rest, ysb, x2, conv_w, conv_b, dt_bias, a_log, d_exp, norm_w, sc_w, w_out, tri,
      norm2_w, wk_hi, wk_lo)


def _sort16_pairs():
    def merge(lo, hi, r):
        step = r * 2
        if step < hi - lo:
            yield from merge(lo, hi, step)
            yield from merge(lo + r, hi, step)
            for i in range(lo + r, hi - r, step):
                yield (i, i + r)
        else:
            yield (lo, lo + r)

    def sort(lo, hi):
        if hi - lo >= 1:
            mid = lo + (hi - lo) // 2
            yield from sort(lo, mid)
            yield from sort(mid + 1, hi)
            yield from merge(lo, hi, 1)

    return tuple(sort(0, PEER_TOPK - 1))


_SORT16 = _sort16_pairs()


def _exchange(v, i, j):
    v[i], v[j] = jnp.maximum(v[i], v[j]), jnp.minimum(v[i], v[j])


def _merge_sublane_lists(v):
    for shift in (4, 2, 1):
        r = [pltpu.roll(x, shift, axis=0) for x in v]
        v = [jnp.maximum(v[i], r[PEER_TOPK - 1 - i]) for i in range(PEER_TOPK)]
        d = PEER_TOPK // 2
        while d >= 1:
            for i in range(PEER_TOPK):
                if not i & d:
                    _exchange(v, i, i + d)
            d //= 2
    return v


def _top16_keys(scores):
    v = [scores[k * SUBLANES:(k + 1) * SUBLANES, :] for k in range(PEER_NKEYS // SUBLANES)]
    for i, j in _SORT16:
        _exchange(v, i, j)
    return _merge_sublane_lists(v)


def _top16_pairs(top0, top1):
    t = top0[0].shape[1]
    sub = lax.broadcasted_iota(jnp.int32, (SUBLANES, t), 0)
    lengths = (16, 8, 5, 4, 12, 4, 1, 0)
    length = jnp.zeros((SUBLANES, t), jnp.int32)
    for s_, n_ in enumerate(lengths):
        length = jnp.where(sub == s_, n_, length)
    row_part = jnp.where(sub == 0, top0[0], jnp.where(sub == 1, top0[1],
                                                     jnp.where(sub == 2, top0[2], top0[3])))
    col_part = jnp.where(sub == 4, top1[0], jnp.where(sub == 5, top1[1], top1[2]))
    lists = []
    for r in range(PEER_TOPK):
        a = jnp.where(sub < 4, row_part, top0[min(4 + r, PEER_TOPK - 1)])
        b = jnp.where(sub < 4, top1[r], col_part)
        lists.append(jnp.where(r < length, a + b, NEG_INF))
    return _merge_sublane_lists(lists)


def _split_bf16(x):
    hi = x.astype(BF16)
    return hi, (x - hi.astype(F32)).astype(BF16)


def _dot3(a_hi, a_lo, b_hi, b_lo):
    return (jnp.dot(a_hi, b_hi, preferred_element_type=F32)
            + jnp.dot(a_hi, b_lo, preferred_element_type=F32)
            + jnp.dot(a_lo, b_hi, preferred_element_type=F32))


def _fold_keys_kernel(keys_ref, wqt_ref, hi_ref, lo_ref):
    k_hi, k_lo = _split_bf16(keys_ref[...])
    w_hi, w_lo = _split_bf16(wqt_ref[...])
    hi_ref[...], lo_ref[...] = _split_bf16(_dot3(k_hi, k_lo, w_hi, w_lo))


def _fold_keys(wq_t, keys):
    half = PEER_KEY_DIM // 2
    nblk = wq_t.shape[0] // half
    out = jax.ShapeDtypeStruct((nblk * PEER_NKEYS, D_MODEL), BF16)
    out_spec = pl.BlockSpec((PEER_NKEYS, D_MODEL), lambda i: (i, 0))
    return pl.pallas_call(
        _fold_keys_kernel,
        grid=(nblk,),
        in_specs=[
            pl.BlockSpec((None, PEER_NKEYS, half), lambda i: (i % 2, 0, 0)),
            pl.BlockSpec((half, D_MODEL), lambda i: (i, 0)),
        ],
        out_specs=[out_spec, out_spec],
        out_shape=[out, out],
        compiler_params=_params(
            ("parallel",),
            _nbytes((PEER_NKEYS, half), F32) + _nbytes((half, D_MODEL), F32)
            + 2 * _nbytes((PEER_NKEYS, D_MODEL), BF16),
            2 * _nbytes((PEER_NKEYS, D_MODEL), F32)),
        name="peer_fold_keys",
    )(keys, wq_t)


def _route_kernel(x_ref, nw_ref, wk_hi_ref, wk_lo_ref, ht_ref, n0_ref, w0_ref, r1_ref, e1_ref):
    h = _rmsnorm(x_ref[...], nw_ref[...])
    ht_hi, ht_lo = _split_bf16(h.T)
    ht_ref[...] = ht_hi
    scores = _dot3(wk_hi_ref[...], wk_lo_ref[...], ht_hi, ht_lo)
    for hh in range(PEER_HEADS):
        s0 = scores[(2 * hh) * PEER_NKEYS:(2 * hh + 1) * PEER_NKEYS, :]
        s1 = scores[(2 * hh + 1) * PEER_NKEYS:(2 * hh + 2) * PEER_NKEYS, :]
        for l0 in range(0, scores.shape[1], LANES):
            ln = slice(l0, l0 + LANES)
            _route_lane_tile(s0[:, ln], s1[:, ln], hh, ln, n0_ref, w0_ref, r1_ref, e1_ref)


def _count_leading(t, pred):
    c8 = pred(t[7])
    c4 = pred(jnp.where(c8, t[11], t[3]))
    c2 = pred(jnp.where(c8, jnp.where(c4, t[13], t[9]), jnp.where(c4, t[5], t[1])))
    c1 = pred(jnp.where(c8,
                        jnp.where(c4, jnp.where(c2, t[14], t[12]), jnp.where(c2, t[10], t[8])),
                        jnp.where(c4, jnp.where(c2, t[6], t[4]), jnp.where(c2, t[2], t[0]))))
    count = (jnp.where(c8, 8.0, 0.0) + jnp.where(c4, 4.0, 0.0)
             + jnp.where(c2, 2.0, 0.0) + jnp.where(c1, 1.0, 0.0))
    return count + jnp.where(pred(t[15]), 1.0, 0.0)


def _route_lane_tile(s0, s1, hh, ln, n0_ref, w0_ref, r1_ref, e1_ref):
    top0 = _top16_keys(s0)
    top1 = _top16_keys(s1)
    best = _top16_pairs(top0, top1)
    z = jnp.exp(best[0] - best[0])
    for r in range(1, PEER_TOPK):
        z = z + jnp.exp(best[r] - best[0])
    tau = best[PEER_TOPK - 1]
    scale0 = INV_SQRT2 / z
    r1_rows, e1_rows = [], []
    for k in range(PEER_NKEYS // SUBLANES):
        rows = slice(k * SUBLANES, (k + 1) * SUBLANES)
        s0k = s0[rows, :]
        s1k = s1[rows, :]
        n0_ref[hh, rows, ln] = _count_leading(top1, lambda m: s0k + m >= tau)
        w0_ref[hh, rows, ln] = jnp.exp(s0k - top0[0]) * scale0
        r1_rows.append(_count_leading(top1, lambda m: m > s1k))
        e1_rows.append(jnp.exp(s1k - top1[0]))
    r1_ref[hh, :, ln] = jnp.concatenate(r1_rows, axis=0).astype(BF16)
    e1_ref[hh, :, ln] = jnp.concatenate(e1_rows, axis=0).astype(BF16)


def _bf16_rows(row, n):
    packed = 2 * SUBLANES
    tile = jnp.broadcast_to(row, (packed, row.shape[1])).astype(BF16)
    return jnp.concatenate([tile] * (n // packed), axis=0)


def _expert_kernel(ht_ref, n0_ref, w0_ref, r1_ref, e1_ref, u_ref, vt_ref, x_ref, fw_ref,
                   o_ref, acc_ref, *, final_norm):
    e = pl.program_id(1)
    eb = u_ref.shape[0]
    t = ht_ref.shape[1]

    @pl.when(e == 0)
    def _():
        acc_ref[...] = jnp.zeros_like(acc_ref)

    def gated(y, sub):
        gate = jnp.zeros((PEER_NKEYS, t), BF16)
        for hh in range(PEER_HEADS):
            n0 = _bf16_rows(n0_ref[hh, sub:sub + 1, :], PEER_NKEYS)
            w0 = _bf16_rows(w0_ref[hh, sub:sub + 1, :], PEER_NKEYS)
            gate = gate + jnp.where(r1_ref[hh] < n0, e1_ref[hh], 0.0) * w0
        return gate * (y * (1.0 + lax.erf(y))).astype(BF16)

    half = eb // 2
    nsub = half // PEER_NKEYS
    ht = ht_ref[...]
    y_a = jnp.dot(u_ref[0:half, :], ht, preferred_element_type=F32)
    rows_a = [gated(y_a[s * PEER_NKEYS:(s + 1) * PEER_NKEYS, :], s) for s in range(nsub)]
    bits = pltpu.bitcast(rows_a[0][0:2 * SUBLANES, 0:LANES], jnp.uint32)
    sixteen = jnp.uint32(16)
    zero = pltpu.bitcast(lax.shift_right_logical(lax.shift_right_logical(bits, sixteen), sixteen), BF16)
    ht_b = ht + jnp.tile(zero, (ht.shape[0] // zero.shape[0], t // LANES))
    y_b = jnp.dot(u_ref[half:eb, :], ht_b, preferred_element_type=F32)
    rows_b = [gated(y_b[s * PEER_NKEYS:(s + 1) * PEER_NKEYS, :], nsub + s) for s in range(nsub)]
    weighted = jnp.concatenate(rows_a + rows_b, axis=0)
    acc_ref[...] += jnp.dot(vt_ref[...], weighted, preferred_element_type=F32)

    @pl.when(e == pl.num_programs(1) - 1)
    def _():
        out = x_ref[...] + acc_ref[...].T
        if final_norm:
            out = _rmsnorm(out, fw_ref[...])
        o_ref[...] = out


def _experts(ht, n0, w0, r1, e1, u_bf, vt_bf, x2, final_w, final_norm):
    n = ht.shape[1]
    t = PEER_TOKENS
    eb = PEER_EXPERTS_PER_STEP
    ne = u_bf.shape[0] // eb
    hk_spec = pl.BlockSpec((PEER_HEADS, PEER_NKEYS, t), lambda i, e: (0, 0, i))
    ha_spec = pl.BlockSpec((PEER_HEADS, eb // PEER_NKEYS, t), lambda i, e: (0, e, i))
    return pl.pallas_call(
        functools.partial(_expert_kernel, final_norm=final_norm),
        grid=(n // t, ne),
        in_specs=[
            pl.BlockSpec((D_MODEL, t), lambda i, e: (0, i)),
            ha_spec, ha_spec, hk_spec, hk_spec,
            pl.BlockSpec((eb, D_MODEL), lambda i, e: (e, 0)),
            pl.BlockSpec((D_MODEL, eb), lambda i, e: (0, e)),
            pl.BlockSpec((t, D_MODEL), lambda i, e: (i, 0)),
            pl.BlockSpec((1, D_MODEL), lambda i, e: (0, 0)),
        ],
        out_specs=pl.BlockSpec((t, D_MODEL), lambda i, e: (i, 0)),
        out_shape=jax.ShapeDtypeStruct((n, D_MODEL), F32),
        scratch_shapes=[pltpu.VMEM((D_MODEL, t), F32)],
        compiler_params=_params(
            ("parallel", "arbitrary"),
            _nbytes((D_MODEL, t), BF16) + 2 * _nbytes((PEER_HEADS, eb // PEER_NKEYS, t), F32)
            + 2 * _nbytes((PEER_HEADS, PEER_NKEYS, t), BF16) + 2 * _nbytes((eb, D_MODEL), BF16)
            + 2 * _nbytes((t, D_MODEL), F32),
            _nbytes((D_MODEL, t), F32) + _nbytes((eb, t), F32) + _nbytes((eb, t), BF16)),
        name="peer_experts",
    )(ht, n0, w0, r1, e1, u_bf, vt_bf, x2, final_w)


def _pad_lanes(v, width):
    return jnp.pad(v, ((0, 0), (0, width - v.shape[1])))


def kernel(x, norm1_w, w_in, ssm_conv_w, ssm_conv_b, ssm_dt_bias, ssm_A_log, ssm_D, ssm_norm_w,
           sc_conv_w, w_out, norm2_w, peer_w_q, peer_sub_keys, peer_u, peer_v, final_norm_w):
    batch, seq, d = x.shape
    depth = w_in.shape[0]
    n = batch * seq
    x2 = x.reshape(n, d)

    cuts = [0]
    for width in (SB_WIDTH, SB_WIDTH, SB_WIDTH, SSM_WIDTH, SSM_CONV_DIM, SSM_HEADS,
                  SC_WIDTH, SC_WIDTH, SC_WIDTH):
        cuts.append(cuts[-1] + width)
    ident = jnp.arange(SB_BLOCK)
    upper = (ident[:, None] > ident[None, :]).astype(BF16)
    identm = jnp.arange(MIX_BLOCK)
    tri = (identm[None, :] <= identm[:, None]).astype(BF16)

    for layer in range(depth):
        w = w_in[layer]
        w_cat = jnp.concatenate([
            w[:, cuts[3]:cuts[5]],
            _pad_lanes(w[:, cuts[5]:cuts[6]], DT_PAD),
            w[:, cuts[6]:cuts[9]],
            w[:, cuts[0]:cuts[3]],
        ], axis=1).astype(BF16)
        rest, qkv = _in_proj(x2, norm1_w[layer][None, :], w_cat)

        ysb = _sb_attention(qkv, batch, upper)

        x2, ht, n0, w0, r1, e1 = _mixer_route(
            rest, ysb, x2, batch,
            ssm_conv_w[layer], ssm_conv_b[layer][None, :],
            _pad_lanes(ssm_dt_bias[layer][None, :], DT_PAD),
            _pad_lanes(ssm_A_log[layer][None, :], DT_PAD),
            jnp.repeat(ssm_D[layer], SSM_HEAD_DIM)[None, :],
            ssm_norm_w[layer][None, :], sc_conv_w[layer],
            w_out[layer].astype(BF16), tri,
            norm2_w[layer][None, :], peer_w_q[layer].T, peer_sub_keys[layer])
        x2 = _experts(ht, n0, w0, r1, e1,
                      (peer_u[layer] * INV_SQRT2).astype(BF16), peer_v[layer].T.astype(BF16),
                      x2, final_norm_w[None, :], layer == depth - 1)
    return x2.reshape(batch, seq, d)
```
